```python
import math
import jax, jax.numpy as jnp
from jax import lax
import numpy as np

D_MODEL = 2048
BATCH = 4
SEQ = 2048
DEPTH = 1
DEC_BATCH = 128
DEC_SEQ = 1
PAST_LEN = 16384
PAGE_SIZE = 128

RET_HEADS = 8
RET_DK = 128
RET_DV = 128
RET_QK = RET_HEADS * RET_DK
RET_WIDTH = RET_HEADS * RET_DV
RET_CHUNK = 128
GM_GROUPS = 8
GM_GROUP_DIM = 128
GM_WIDTH = GM_GROUPS * GM_GROUP_DIM
GM_CHUNK = 128
MIX_WIDTH = RET_WIDTH + GM_WIDTH
IN_COLS = 2 * RET_QK + 2 * RET_WIDTH + 2 * GM_WIDTH
D_FF = 5632
CONV_W = 3
PLE_DIM = 256
ROPE_THETA = 10000.0
EPS = 1e-6

kernel_name = "hybrid_retention_chunkgmlp_convffn_step"

_SPLITS = [RET_QK, 2 * RET_QK, 2 * RET_QK + RET_WIDTH, 2 * RET_QK + 2 * RET_WIDTH,
           2 * RET_QK + 2 * RET_WIDTH + GM_WIDTH]


def _rmsnorm(x, g):
    x32 = x.astype(jnp.float32)
    y = x32 * lax.rsqrt(jnp.mean(x32 * x32, axis=-1, keepdims=True) + EPS)
    return (y * g.astype(jnp.float32)).astype(x.dtype)


def _layernorm(x, g, b):
    x32 = x.astype(jnp.float32)
    mu = jnp.mean(x32, axis=-1, keepdims=True)
    xc = x32 - mu
    y = xc * lax.rsqrt(jnp.mean(xc * xc, axis=-1, keepdims=True) + EPS)
    return (y * g.astype(jnp.float32) + b.astype(jnp.float32)).astype(x.dtype)


def _rope(x, pos):
    dh = x.shape[-1]
    half = dh // 2
    inv = ROPE_THETA ** (-jnp.arange(0, dh, 2, dtype=jnp.float32) / dh)
    ang = pos.astype(jnp.float32)[:, None] * inv[None, :]
    cos = jnp.cos(ang)[None, :, None, :]
    sin = jnp.sin(ang)[None, :, None, :]
    x1, x2 = x[..., :half], x[..., half:]
    return jnp.concatenate([x1 * cos - x2 * sin, x2 * cos + x1 * sin], axis=-1)


def _ret_chunk_len(L):
    if L % RET_CHUNK == 0:
        return RET_CHUNK
    if L < RET_CHUNK:
        return L
    return math.gcd(L, RET_CHUNK)


def _retention(q, k, v, S0):
    B, L, H, _ = q.shape
    c = _ret_chunk_len(L)
    n = L // c
    log_g = jnp.log1p(-jnp.exp2(-5.0 - jnp.arange(H, dtype=jnp.float32)))
    idx = jnp.arange(c, dtype=jnp.float32)
    diff = idx[:, None] - idx[None, :]
    dmask = jnp.where(diff >= 0, jnp.exp(log_g[:, None, None] * jnp.maximum(diff, 0.0)), 0.0)
    q_dec = jnp.exp(log_g[:, None] * (idx + 1.0))[..., None]
    k_dec = jnp.exp(log_g[:, None] * (c - 1.0 - idx))[..., None]
    c_dec = jnp.exp(log_g * c)[:, None, None]

    def to_chunks(t):
        return t.reshape(B, n, c, H, t.shape[-1]).transpose(1, 0, 3, 2, 4)

    def step(S, xs):
        qc, kc, vc = xs
        sc = jnp.einsum('bhid,bhjd->bhij', qc, kc) * dmask
        o = (jnp.einsum('bhij,bhje->bhie', sc, vc)
             + jnp.einsum('bhid,bhde->bhie', qc * q_dec, S))
        S = S * c_dec + jnp.einsum('bhjd,bhje->bhde', kc * k_dec, vc)
        return S, o

    S, o = lax.scan(step, S0, (to_chunks(q), to_chunks(k), to_chunks(v)))
    o = o.transpose(1, 0, 3, 2, 4).reshape(B, L, H, -1)
    return o, S


def _chunk_spatial(vn, ws, bs):
    B, L, C = vn.shape
    c = L if L <= GM_CHUNK else GM_CHUNK
    Lp = -(-L // c) * c
    if Lp != L:
        vn = jnp.pad(vn, ((0, 0), (0, Lp - L), (0, 0)))
    n = Lp // c
    vr = vn.reshape(B, n, c, GM_GROUPS, GM_GROUP_DIM)
    mask = jnp.tril(jnp.ones((c, c), dtype=bool))
    wm = jnp.where(mask[None], ws[:, :c, :c], 0.0).astype(vn.dtype)
    out = jnp.einsum('gts,bnsgd->bntgd', wm, vr) + bs[:, :c].T.astype(vn.dtype)[None, None, :, :, None]
    return out.reshape(B, Lp, C)[:, :L]


def _layer(h, p, pos, S0, conv_buf, g_attn, w_in, gm_ln_g, gm_ln_b, gm_ws, gm_bs, w_o,
           g_ffn, w_up, conv_w, conv_b, w_down, g_ple, w_ple_gate, w_ple_proj):
    B, L, _ = h.shape
    dt = h.dtype
    xn = _rmsnorm(h, g_attn)
    z = xn @ w_in
    q, k, v, g, u, vg = jnp.split(z, _SPLITS, axis=-1)
    q = _rope(q.reshape(B, L, RET_HEADS, RET_DK).astype(jnp.float32), pos)
    k = _rope(k.reshape(B, L, RET_HEADS, RET_DK).astype(jnp.float32), pos) * (RET_DK ** -0.5)
    v = v.reshape(B, L, RET_HEADS, RET_DV).astype(jnp.float32)
    o, S_new = _retention(q, k, v, S0.astype(jnp.float32))
    o = o * lax.rsqrt(jnp.mean(o * o, axis=-1, keepdims=True) + EPS)
    ret_out = o.reshape(B, L, RET_WIDTH).astype(dt) * jax.nn.silu(g)
    u = jax.nn.gelu(u)
    vn = _layernorm(jax.nn.gelu(vg), gm_ln_g, gm_ln_b)
    gm_out = u * _chunk_spatial(vn, gm_ws, gm_bs)
    h = h + jnp.concatenate([ret_out, gm_out], axis=-1) @ w_o
    a = _rmsnorm(h, g_ffn) @ w_up
    a_ext = jnp.concatenate([conv_buf.astype(dt), a], axis=1)
    ac = (a_ext[:, 0:L] * conv_w[0] + a_ext[:, 1:L + 1] * conv_w[1]
          + a_ext[:, 2:L + 2] * conv_w[2] + conv_b)
    gate, up = jnp.split(ac, 2, axis=-1)
    h = h + (jax.nn.silu(gate) * up) @ w_down
    conv_new = a_ext[:, -(CONV_W - 1):]
    h = h + (p.astype(dt) @ w_ple_proj) * jax.nn.sigmoid(_rmsnorm(h, g_ple) @ w_ple_gate)
    return h, S_new, conv_new, vn


def setup_inputs(seed: int = 0) -> dict:
    key = jax.random.key(seed)
    ks = jax.random.split(key, 24)
    f32 = jnp.float32
    nrm = lambda k, shp, s: jax.random.normal(k, shp, f32) * s
    return {
        "x_prompt": nrm(ks[0], (BATCH, SEQ, D_MODEL), 1.0),
        "x_sample": nrm(ks[1], (DEC_BATCH, DEC_SEQ, D_MODEL), 1.0),
        "p_prompt": nrm(ks[2], (DEPTH, BATCH, SEQ, PLE_DIM), 1.0),
        "p_sample": nrm(ks[3], (DEPTH, DEC_BATCH, DEC_SEQ, PLE_DIM), 1.0),
        "state_ret": nrm(ks[4], (DEPTH, DEC_BATCH, RET_HEADS, RET_DK, RET_DV), 0.3),
        "state_conv": nrm(ks[5], (DEPTH, DEC_BATCH, CONV_W - 1, 2 * D_FF), 1.0),
        "g_attn": 1.0 + nrm(ks[6], (DEPTH, D_MODEL), 0.02),
        "w_in": nrm(ks[7], (DEPTH, D_MODEL, IN_COLS), D_MODEL ** -0.5),
        "gm_ln_g": 1.0 + nrm(ks[8], (DEPTH, GM_WIDTH), 0.02),
        "gm_ln_b": nrm(ks[9], (DEPTH, GM_WIDTH), 0.02),
        "gm_ws": nrm(ks[10], (DEPTH, GM_GROUPS, GM_CHUNK, GM_CHUNK), GM_CHUNK ** -0.5),
        "gm_bs": 1.0 + nrm(ks[11], (DEPTH, GM_GROUPS, GM_CHUNK), 0.02),
        "w_o": nrm(ks[12], (DEPTH, MIX_WIDTH, D_MODEL), MIX_WIDTH ** -0.5),
        "g_ffn": 1.0 + nrm(ks[13], (DEPTH, D_MODEL), 0.02),
        "w_up": nrm(ks[14], (DEPTH, D_MODEL, 2 * D_FF), D_MODEL ** -0.5),
        "conv_w": nrm(ks[15], (DEPTH, CONV_W, 2 * D_FF), CONV_W ** -0.5),
        "conv_b": nrm(ks[16], (DEPTH, 2 * D_FF), 0.02),
        "w_down": nrm(ks[17], (DEPTH, D_FF, D_MODEL), D_FF ** -0.5),
        "g_ple": 1.0 + nrm(ks[18], (DEPTH, D_MODEL), 0.02),
        "w_ple_gate": nrm(ks[19], (DEPTH, D_MODEL, D_MODEL), D_MODEL ** -0.5),
        "w_ple_proj": nrm(ks[20], (DEPTH, PLE_DIM, D_MODEL), PLE_DIM ** -0.5),
        "g_final": 1.0 + nrm(ks[21], (D_MODEL,), 0.02),
    }


def reference(x_prompt, x_sample, p_prompt, p_sample, state_ret, state_conv,
              g_attn, w_in, gm_ln_g, gm_ln_b, gm_ws, gm_bs, w_o, g_ffn, w_up, conv_w, conv_b,
              w_down, g_ple, w_ple_gate, w_ple_proj, g_final):
    Bp, Lp, _ = x_prompt.shape
    Bs, Ls, _ = x_sample.shape
    pos_p = jnp.arange(Lp, dtype=jnp.int32)
    pos_s = PAST_LEN + jnp.arange(Ls, dtype=jnp.int32)
    hp, hs = x_prompt, x_sample
    ret_p, conv_p, ret_s, conv_s, gmv_s = [], [], [], [], []
    for i in range(DEPTH):
        w = (g_attn[i], w_in[i], gm_ln_g[i], gm_ln_b[i], gm_ws[i], gm_bs[i], w_o[i],
             g_ffn[i], w_up[i], conv_w[i], conv_b[i], w_down[i], g_ple[i], w_ple_gate[i], w_ple_proj[i])
        S0p = jnp.zeros((Bp, RET_HEADS, RET_DK, RET_DV), jnp.float32)
        cb0p = jnp.zeros((Bp, CONV_W - 1, 2 * D_FF), x_prompt.dtype)
        hp, Sp, cp, _ = _layer(hp, p_prompt[i], pos_p, S0p, cb0p, *w)
        hs, Ss, cs, vs = _layer(hs, p_sample[i], pos_s, state_ret[i], state_conv[i], *w)
        ret_p.append(Sp.astype(state_ret.dtype))
        conv_p.append(cp.astype(state_conv.dtype))
        ret_s.append(Ss.astype(state_ret.dtype))
        conv_s.append(cs.astype(state_conv.dtype))
        gmv_s.append(vs)
    y_prompt = _rmsnorm(hp, g_final)
    y_sample = _rmsnorm(hs, g_final)
    return (y_prompt, y_sample, jnp.stack(ret_p), jnp.stack(conv_p),
            jnp.stack(ret_s), jnp.stack(conv_s), jnp.stack(gmv_s))
```

```python
import functools
import math

import jax
import jax.numpy as jnp
from jax import lax
from jax.experimental import pallas as pl
from jax.experimental.pallas import tpu as pltpu

F32 = jnp.float32
BF16 = jnp.bfloat16

D_MODEL = 2048
RET_HEADS = 8
HEAD_DIM = 128
CHUNK = 128
RET_QK = RET_HEADS * HEAD_DIM
GM_WIDTH = 1024
IN_COLS = 6144
D_FF = 5632
PLE_DIM = 256
ROPE_THETA = 10000.0
PAST_LEN = 16384
EPS = 1e-6
K_SCALE = HEAD_DIM ** -0.5

OFF_Q, OFF_K, OFF_V, OFF_G, OFF_U, OFF_VG = 0, 1024, 2048, 3072, 4096, 5120

VMEM_LIMIT = 56 * 1024 * 1024


def _params(semantics):
    return pltpu.CompilerParams(dimension_semantics=semantics, vmem_limit_bytes=VMEM_LIMIT)


def _rms_rows(x, g):
    ms = jnp.mean(x * x, axis=-1, keepdims=True)
    return x * lax.rsqrt(ms + EPS) * g


def _gelu(x):
    return jax.nn.gelu(x)


def _silu(x):
    return x * jax.nn.sigmoid(x)


def _norm_matmul_kernel(x_ref, g_ref, w_ref, o_ref, xn_ref, *, row_chunk):
    @pl.when(pl.program_id(1) == 0)
    def _():
        def body(r, carry):
            rows = pl.ds(pl.multiple_of(r * row_chunk, row_chunk), row_chunk)
            xn_ref[rows, :] = _rms_rows(x_ref[rows, :], g_ref[...]).astype(BF16)
            return carry
        lax.fori_loop(0, x_ref.shape[0] // row_chunk, body, 0)

    o_ref[...] = jnp.dot(xn_ref[...], w_ref[...], preferred_element_type=F32)


def _norm_matmul(x, g, w, *, tm, tn, name):
    m, k = x.shape
    n = w.shape[1]
    row_chunk = min(tm, 32)
    return pl.pallas_call(
        functools.partial(_norm_matmul_kernel, row_chunk=row_chunk),
        grid=(m // tm, n // tn),
        in_specs=[
            pl.BlockSpec((tm, k), lambda i, j: (i, 0)),
            pl.BlockSpec((1, k), lambda i, j: (0, 0)),
            pl.BlockSpec((k, tn), lambda i, j: (0, j)),
        ],
        out_specs=pl.BlockSpec((tm, tn), lambda i, j: (i, j)),
        out_shape=jax.ShapeDtypeStruct((m, n), F32),
        scratch_shapes=[pltpu.VMEM((tm, k), BF16)],
        compiler_params=_params(("arbitrary", "arbitrary")),
        name=name,
    )(x, g, w)


def _rope(x, cosf, sinf):
    return x * cosf + pltpu.roll(x, HEAD_DIM // 2, 1) * sinf


def _layernorm_rows(x, g, b):
    mu = jnp.mean(x, axis=-1, keepdims=True)
    xc = x - mu
    var = jnp.mean(xc * xc, axis=-1, keepdims=True)
    return xc * lax.rsqrt(var + EPS) * g + b


def _prompt_mixer_kernel(z_ref, cos_ref, sin_ref, dmask_ref, qdec_ref, kdec_ref, cdec_ref,
                         lng_ref, lnb_ref, ws_ref, bsf_ref, mix_ref, s_ref):
    @pl.when(pl.program_id(1) == 0)
    def _():
        s_ref[...] = jnp.zeros_like(s_ref)

    cosf = cos_ref[...]
    sinf = sin_ref[...]
    for h in range(RET_HEADS):
        cols = slice(h * HEAD_DIM, (h + 1) * HEAD_DIM)
        q = _rope(z_ref[:, OFF_Q + h * HEAD_DIM:OFF_Q + (h + 1) * HEAD_DIM], cosf, sinf)
        k = _rope(z_ref[:, OFF_K + h * HEAD_DIM:OFF_K + (h + 1) * HEAD_DIM], cosf, sinf) * K_SCALE
        v = z_ref[:, OFF_V + h * HEAD_DIM:OFF_V + (h + 1) * HEAD_DIM].astype(BF16)
        g = z_ref[:, OFF_G + h * HEAD_DIM:OFF_G + (h + 1) * HEAD_DIM]
        s_old = s_ref[0, h]
        sc = lax.dot_general(q.astype(BF16), k.astype(BF16), (((1,), (1,)), ((), ())),
                             preferred_element_type=F32) * dmask_ref[h]
        o = (jnp.dot(sc.astype(BF16), v, preferred_element_type=F32)
             + jnp.dot((q * qdec_ref[h]).astype(BF16), s_old.astype(BF16),
                       preferred_element_type=F32))
        s_ref[0, h] = s_old * cdec_ref[h] + lax.dot_general(
            (k * kdec_ref[h]).astype(BF16), v, (((0,), (0,)), ((), ())),
            preferred_element_type=F32)
        o = o * lax.rsqrt(jnp.mean(o * o, axis=-1, keepdims=True) + EPS)
        mix_ref[:, cols] = (o * _silu(g)).astype(mix_ref.dtype)

    vn = _layernorm_rows(_gelu(z_ref[:, OFF_VG:OFF_VG + GM_WIDTH]), lng_ref[...], lnb_ref[...])
    row = lax.broadcasted_iota(jnp.int32, (CHUNK, CHUNK), 0)
    col = lax.broadcasted_iota(jnp.int32, (CHUNK, CHUNK), 1)
    causal = row >= col
    for grp in range(GM_WIDTH // HEAD_DIM):
        cols = slice(grp * HEAD_DIM, (grp + 1) * HEAD_DIM)
        wm = jnp.where(causal, ws_ref[grp], 0.0).astype(BF16)
        sp = jnp.dot(wm, vn[:, cols].astype(BF16), preferred_element_type=F32) + bsf_ref[:, cols]
        u = _gelu(z_ref[:, OFF_U + grp * HEAD_DIM:OFF_U + (grp + 1) * HEAD_DIM])
        mix_ref[:, RET_QK + grp * HEAD_DIM:RET_QK + (grp + 1) * HEAD_DIM] = (u * sp).astype(mix_ref.dtype)


def _prompt_mixer(z, batch, seq, cosf, sinf, dmask, qdec, kdec, cdec, lng, lnb, ws, bsf):
    n_chunks = seq // CHUNK
    const3 = lambda b, c: (0, 0, 0)
    const2 = lambda b, c: (0, 0)
    return pl.pallas_call(
        _prompt_mixer_kernel,
        grid=(batch, n_chunks),
        in_specs=[
            pl.BlockSpec((CHUNK, IN_COLS), lambda b, c: (b * n_chunks + c, 0)),
            pl.BlockSpec((CHUNK, HEAD_DIM), lambda b, c: (c, 0)),
            pl.BlockSpec((CHUNK, HEAD_DIM), lambda b, c: (c, 0)),
            pl.BlockSpec((RET_HEADS, CHUNK, CHUNK), const3),
            pl.BlockSpec((RET_HEADS, CHUNK, HEAD_DIM), const3),
            pl.BlockSpec((RET_HEADS, CHUNK, HEAD_DIM), const3),
            pl.BlockSpec((RET_HEADS, 1, HEAD_DIM), const3),
            pl.BlockSpec((1, GM_WIDTH), const2),
            pl.BlockSpec((1, GM_WIDTH), const2),
            pl.BlockSpec((GM_WIDTH // HEAD_DIM, CHUNK, CHUNK), const3),
            pl.BlockSpec((CHUNK, GM_WIDTH), const2),
        ],
        out_specs=[
            pl.BlockSpec((CHUNK, D_MODEL), lambda b, c: (b * n_chunks + c, 0)),
            pl.BlockSpec((1, RET_HEADS, HEAD_DIM, HEAD_DIM), lambda b, c: (b, 0, 0, 0)),
        ],
        out_shape=[
            jax.ShapeDtypeStruct((batch * seq, D_MODEL), BF16),
            jax.ShapeDtypeStruct((batch, RET_HEADS, HEAD_DIM, HEAD_DIM), F32),
        ],
        compiler_params=_params(("arbitrary", "arbitrary")),
        name="prompt_mixer",
    )(z, cosf, sinf, dmask, qdec, kdec, cdec, lng, lnb, ws, bsf)


def _sample_mixer_kernel(z_ref, s_ref, cos_ref, sin_ref, dmask_ref, qdec_ref, kdec_ref, cdec_ref,
                         lng_ref, lnb_ref, w00_ref, bs0_ref, mix_ref, so_ref, vn_ref, *, tb):
    cosf = cos_ref[...]
    sinf = sin_ref[...]
    row = lax.broadcasted_iota(jnp.int32, (HEAD_DIM, HEAD_DIM), 0)
    col = lax.broadcasted_iota(jnp.int32, (HEAD_DIM, HEAD_DIM), 1)
    eye = row == col
    ones = jnp.ones((HEAD_DIM, HEAD_DIM), BF16)

    def lane_broadcast_columns(rows):
        diag = jnp.concatenate(
            [jnp.where(eye, jnp.broadcast_to(rows[i:i + 1, :], (HEAD_DIM, HEAD_DIM)), 0.0)
             for i in range(rows.shape[0])], axis=0)
        hi = diag.astype(BF16)
        lo = (diag - hi.astype(F32)).astype(BF16)
        return (jnp.dot(hi, ones, preferred_element_type=F32)
                + jnp.dot(lo, ones, preferred_element_type=F32))

    for h in range(RET_HEADS):
        cols = slice(h * HEAD_DIM, (h + 1) * HEAD_DIM)
        q = _rope(z_ref[:, OFF_Q + h * HEAD_DIM:OFF_Q + (h + 1) * HEAD_DIM], cosf, sinf)
        k = _rope(z_ref[:, OFF_K + h * HEAD_DIM:OFF_K + (h + 1) * HEAD_DIM], cosf, sinf) * K_SCALE
        v = z_ref[:, OFF_V + h * HEAD_DIM:OFF_V + (h + 1) * HEAD_DIM]
        g = z_ref[:, OFF_G + h * HEAD_DIM:OFF_G + (h + 1) * HEAD_DIM]
        qdec = qdec_ref[h]
        kdec = kdec_ref[h]
        cdec = cdec_ref[h]
        sc = jnp.sum(q * k, axis=-1, keepdims=True) * dmask_ref[h]
        qk_cols = lane_broadcast_columns(jnp.concatenate([q * qdec, k * kdec], axis=0))
        o_rows = []
        for b in range(tb):
            s_old = s_ref[b, h]
            q_col = qk_cols[b * HEAD_DIM:(b + 1) * HEAD_DIM, :]
            k_col = qk_cols[(tb + b) * HEAD_DIM:(tb + b + 1) * HEAD_DIM, :]
            v_row = v[b:b + 1, :]
            o_rows.append(jnp.sum(q_col * s_old, axis=0, keepdims=True))
            so_ref[b, h] = s_old * cdec + k_col * v_row
        o = sc * v + jnp.concatenate(o_rows, axis=0)
        o = o * lax.rsqrt(jnp.mean(o * o, axis=-1, keepdims=True) + EPS)
        mix_ref[:, cols] = o * _silu(g)

    vn = _layernorm_rows(_gelu(z_ref[:, OFF_VG:OFF_VG + GM_WIDTH]), lng_ref[...], lnb_ref[...])
    vn_ref[...] = vn
    u = _gelu(z_ref[:, OFF_U:OFF_U + GM_WIDTH])
    mix_ref[:, RET_QK:RET_QK + GM_WIDTH] = u * (w00_ref[...] * vn + bs0_ref[...])


def _sample_mixer(z, state, cosf, sinf, dmask, qdec, kdec, cdec, lng, lnb, w00, bs0, *, tb):
    nb = z.shape[0]
    const3 = lambda i: (0, 0, 0)
    const2 = lambda i: (0, 0)
    state_spec = pl.BlockSpec((tb, RET_HEADS, HEAD_DIM, HEAD_DIM), lambda i: (i, 0, 0, 0))
    return pl.pallas_call(
        functools.partial(_sample_mixer_kernel, tb=tb),
        grid=(nb // tb,),
        in_specs=[
            pl.BlockSpec((tb, IN_COLS), lambda i: (i, 0)),
            state_spec,
            pl.BlockSpec((1, HEAD_DIM), const2),
            pl.BlockSpec((1, HEAD_DIM), const2),
            pl.BlockSpec((RET_HEADS, 1, 1), const3),
            pl.BlockSpec((RET_HEADS, 1, HEAD_DIM), const3),
            pl.BlockSpec((RET_HEADS, 1, HEAD_DIM), const3),
            pl.BlockSpec((RET_HEADS, 1, HEAD_DIM), const3),
            pl.BlockSpec((1, GM_WIDTH), const2),
            pl.BlockSpec((1, GM_WIDTH), const2),
            pl.BlockSpec((1, GM_WIDTH), const2),
            pl.BlockSpec((1, GM_WIDTH), const2),
        ],
        out_specs=[
            pl.BlockSpec((tb, D_MODEL), lambda i: (i, 0)),
            state_spec,
            pl.BlockSpec((tb, GM_WIDTH), lambda i: (i, 0)),
        ],
        out_shape=[
            jax.ShapeDtypeStruct((nb, D_MODEL), F32),
            jax.ShapeDtypeStruct(state.shape, F32),
            jax.ShapeDtypeStruct((nb, GM_WIDTH), F32),
        ],
        compiler_params=_params(("arbitrary",)),
        name="sample_mixer",
    )(z, state, cosf, sinf, dmask, qdec, kdec, cdec, lng, lnb, w00, bs0)


def _attn_out_kernel(mix_ref, w_ref, h_ref, g_ref, h1_ref, xn_ref, *, row_chunk):
    h1_ref[...] = h_ref[...] + jnp.dot(mix_ref[...].astype(BF16), w_ref[...],
                                       preferred_element_type=F32)

    def body(r, carry):
        rows = pl.ds(pl.multiple_of(r * row_chunk, row_chunk), row_chunk)
        xn_ref[rows, :] = _rms_rows(h1_ref[rows, :], g_ref[...]).astype(BF16)
        return carry
    lax.fori_loop(0, h1_ref.shape[0] // row_chunk, body, 0)


def _attn_out(mix, w, h, g, *, tm, name):
    m, d = h.shape
    row_chunk = min(tm, 32)
    return pl.pallas_call(
        functools.partial(_attn_out_kernel, row_chunk=row_chunk),
        grid=(m // tm,),
        in_specs=[
            pl.BlockSpec((tm, d), lambda i: (i, 0)),
            pl.BlockSpec((d, d), lambda i: (0, 0)),
            pl.BlockSpec((tm, d), lambda i: (i, 0)),
            pl.BlockSpec((1, d), lambda i: (0, 0)),
        ],
        out_specs=[
            pl.BlockSpec((tm, d), lambda i: (i, 0)),
            pl.BlockSpec((tm, d), lambda i: (i, 0)),
        ],
        out_shape=[
            jax.ShapeDtypeStruct((m, d), F32),
            jax.ShapeDtypeStruct((m, d), BF16),
        ],
        compiler_params=_params(("arbitrary",)),
        name=name,
    )(mix, w, h, g)


CONV_PAD = 8


def _ffn_up_kernel(xn_ref, wg_ref, wu_ref, cwg_ref, cwu_ref, cbg_ref, cbu_ref,
                   act_ref, csg_ref, csu_ref, ag_ref, au_ref, *, tiles_per_seq, row_chunk):
    i = pl.program_id(1)
    tm = xn_ref.shape[0]
    first = (i % tiles_per_seq) == 0

    @pl.when(first)
    def _():
        ag_ref[0:CONV_PAD, :] = jnp.zeros((CONV_PAD, ag_ref.shape[1]), F32)
        au_ref[0:CONV_PAD, :] = jnp.zeros((CONV_PAD, au_ref.shape[1]), F32)

    @pl.when(jnp.logical_not(first))
    def _():
        ag_ref[0:CONV_PAD, :] = ag_ref[tm:tm + CONV_PAD, :]
        au_ref[0:CONV_PAD, :] = au_ref[tm:tm + CONV_PAD, :]

    xn = xn_ref[...]
    ag_ref[CONV_PAD:CONV_PAD + tm, :] = jnp.dot(xn, wg_ref[...], preferred_element_type=F32)
    au_ref[CONV_PAD:CONV_PAD + tm, :] = jnp.dot(xn, wu_ref[...], preferred_element_type=F32)

    def conv(a_ref, cw_ref, cb_ref, r0):
        a0 = a_ref[r0 + CONV_PAD:r0 + CONV_PAD + row_chunk, :]
        a1 = a_ref[r0 + CONV_PAD - 1:r0 + CONV_PAD - 1 + row_chunk, :]
        a2 = a_ref[r0 + CONV_PAD - 2:r0 + CONV_PAD - 2 + row_chunk, :]
        return a2 * cw_ref[0:1, :] + a1 * cw_ref[1:2, :] + a0 * cw_ref[2:3, :] + cb_ref[...]

    for r in range(tm // row_chunk):
        r0 = r * row_chunk
        gate = conv(ag_ref, cwg_ref, cbg_ref, r0)
        up = conv(au_ref, cwu_ref, cbu_ref, r0)
        act_ref[r0:r0 + row_chunk, :] = (_silu(gate) * up).astype(act_ref.dtype)

    @pl.when((i % tiles_per_seq) == tiles_per_seq - 1)
    def _():
        csg_ref[0] = ag_ref[CONV_PAD + tm - 2:CONV_PAD + tm, :]
        csu_ref[0] = au_ref[CONV_PAD + tm - 2:CONV_PAD + tm, :]


def _ffn_up(xn, w_up, conv_w, conv_b, *, batch, seq, tm, tn):
    m, d = xn.shape
    n_col = D_FF // tn
    tiles_per_seq = seq // tm
    return pl.pallas_call(
        functools.partial(_ffn_up_kernel, tiles_per_seq=tiles_per_seq, row_chunk=64),
        grid=(n_col, m // tm),
        in_specs=[
            pl.BlockSpec((tm, d), lambda j, i: (i, 0)),
            pl.BlockSpec((d, tn), lambda j, i: (0, j)),
            pl.BlockSpec((d, tn), lambda j, i: (0, n_col + j)),
            pl.BlockSpec((3, tn), lambda j, i: (0, j)),
            pl.BlockSpec((3, tn), lambda j, i: (0, n_col + j)),
            pl.BlockSpec((1, tn), lambda j, i: (0, j)),
            pl.BlockSpec((1, tn), lambda j, i: (0, n_col + j)),
        ],
        out_specs=[
            pl.BlockSpec((tm, tn), lambda j, i: (i, j)),
            pl.BlockSpec((1, 2, tn), lambda j, i: (i // tiles_per_seq, 0, j)),
            pl.BlockSpec((1, 2, tn), lambda j, i: (i // tiles_per_seq, 0, j)),
        ],
        out_shape=[
            jax.ShapeDtypeStruct((m, D_FF), BF16),
            jax.ShapeDtypeStruct((batch, 2, D_FF), F32),
            jax.ShapeDtypeStruct((batch, 2, D_FF), F32),
        ],
        scratch_shapes=[pltpu.VMEM((tm + CONV_PAD, tn), F32), pltpu.VMEM((tm + CONV_PAD, tn), F32)],
        compiler_params=_params(("arbitrary", "arbitrary")),
        name="prompt_ffn_up",
    )(xn, w_up, w_up, conv_w, conv_w, conv_b, conv_b)


def _matmul_residual_kernel(a_ref, w_ref, r_ref, o_ref):
    o_ref[...] = r_ref[...] + jnp.dot(a_ref[...], w_ref[...], preferred_element_type=F32)


def _matmul_residual(a, w, res, *, tm, tn, name):
    m, k = a.shape
    n = w.shape[1]
    return pl.pallas_call(
        _matmul_residual_kernel,
        grid=(m // tm, n // tn),
        in_specs=[
            pl.BlockSpec((tm, k), lambda i, j: (i, 0)),
            pl.BlockSpec((k, tn), lambda i, j: (0, j)),
            pl.BlockSpec((tm, tn), lambda i, j: (i, j)),
        ],
        out_specs=pl.BlockSpec((tm, tn), lambda i, j: (i, j)),
        out_shape=jax.ShapeDtypeStruct((m, n), F32),
        compiler_params=_params(("arbitrary", "arbitrary")),
        name=name,
    )(a, w, res)


def _sample_ffn_up_kernel(xn_ref, w_ref, cb0_ref, cb1_ref, cw_ref, cbias_ref, a_ref, ac_ref):
    a = jnp.dot(xn_ref[...], w_ref[...], preferred_element_type=F32)
    a_ref[...] = a
    ac_ref[...] = (cb0_ref[...] * cw_ref[0:1, :] + cb1_ref[...] * cw_ref[1:2, :]
                   + a * cw_ref[2:3, :] + cbias_ref[...])


def _sample_ffn_up(xn, w_up, conv_state2d, conv_w, conv_b, *, tn):
    m, d = xn.shape
    n = w_up.shape[1]
    n_col = n // tn
    return pl.pallas_call(
        _sample_ffn_up_kernel,
        grid=(n_col,),
        in_specs=[
            pl.BlockSpec((m, d), lambda j: (0, 0)),
            pl.BlockSpec((d, tn), lambda j: (0, j)),
            pl.BlockSpec((m, tn), lambda j: (0, j)),
            pl.BlockSpec((m, tn), lambda j: (0, n_col + j)),
            pl.BlockSpec((3, tn), lambda j: (0, j)),
            pl.BlockSpec((1, tn), lambda j: (0, j)),
        ],
        out_specs=[
            pl.BlockSpec((m, tn), lambda j: (0, j)),
            pl.BlockSpec((m, tn), lambda j: (0, j)),
        ],
        out_shape=[
            jax.ShapeDtypeStruct((m, n), F32),
            jax.ShapeDtypeStruct((m, n), F32),
        ],
        compiler_params=_params(("arbitrary",)),
        name="sample_ffn_up",
    )(xn, w_up, conv_state2d, conv_state2d, conv_w, conv_b)


def _sample_ffn_down_kernel(gate_ref, up_ref, w_ref, h_ref, o_ref):
    @pl.when(pl.program_id(0) == 0)
    def _():
        o_ref[...] = h_ref[...]

    act = (_silu(gate_ref[...]) * up_ref[...]).astype(BF16)
    o_ref[...] += jnp.dot(act, w_ref[...], preferred_element_type=F32)


def _sample_ffn_down(ac, w_down, h1, *, tk):
    m, d = h1.shape
    n_k = D_FF // tk
    return pl.pallas_call(
        _sample_ffn_down_kernel,
        grid=(n_k,),
        in_specs=[
            pl.BlockSpec((m, tk), lambda k: (0, k)),
            pl.BlockSpec((m, tk), lambda k: (0, n_k + k)),
            pl.BlockSpec((tk, d), lambda k: (k, 0)),
            pl.BlockSpec((m, d), lambda k: (0, 0)),
        ],
        out_specs=pl.BlockSpec((m, d), lambda k: (0, 0)),
        out_shape=jax.ShapeDtypeStruct((m, d), F32),
        compiler_params=_params(("arbitrary",)),
        name="sample_ffn_down",
    )(ac, ac, w_down, h1)


def _ple_final_kernel(h_ref, p_ref, gple_ref, wg_ref, wp_ref, gfin_ref, y_ref, xn_ref, *, row_chunk):
    n_chunks = h_ref.shape[0] // row_chunk

    def norm_body(r, carry):
        rows = pl.ds(pl.multiple_of(r * row_chunk, row_chunk), row_chunk)
        xn_ref[rows, :] = _rms_rows(h_ref[rows, :], gple_ref[...]).astype(BF16)
        return carry
    lax.fori_loop(0, n_chunks, norm_body, 0)

    gate = jax.nn.sigmoid(jnp.dot(xn_ref[...], wg_ref[...], preferred_element_type=F32))
    proj = jnp.dot(p_ref[...].astype(BF16), wp_ref[...], preferred_element_type=F32)
    y_ref[...] = h_ref[...] + proj * gate

    def final_body(r, carry):
        rows = pl.ds(pl.multiple_of(r * row_chunk, row_chunk), row_chunk)
        y_ref[rows, :] = _rms_rows(y_ref[rows, :], gfin_ref[...])
        return carry
    lax.fori_loop(0, n_chunks, final_body, 0)


def _ple_final(h, p, g_ple, w_gate, w_proj, g_final, *, tm, name):
    m, d = h.shape
    pd = p.shape[1]
    row_chunk = min(tm, 32)
    return pl.pallas_call(
        functools.partial(_ple_final_kernel, row_chunk=row_chunk),
        grid=(m // tm,),
        in_specs=[
            pl.BlockSpec((tm, d), lambda i: (i, 0)),
            pl.BlockSpec((tm, pd), lambda i: (i, 0)),
            pl.BlockSpec((1, d), lambda i: (0, 0)),
            pl.BlockSpec((d, d), lambda i: (0, 0)),
            pl.BlockSpec((pd, d), lambda i: (0, 0)),
            pl.BlockSpec((1, d), lambda i: (0, 0)),
        ],
        out_specs=pl.BlockSpec((tm, d), lambda i: (i, 0)),
        out_shape=jax.ShapeDtypeStruct((m, d), F32),
        scratch_shapes=[pltpu.VMEM((tm, d), BF16)],
        compiler_params=_params(("arbitrary",)),
        name=name,
    )(h, p, g_ple, w_gate, w_proj, g_final)


def _rope_tables(pos):
    half = HEAD_DIM // 2
    inv = ROPE_THETA ** (-jnp.arange(0, HEAD_DIM, 2, dtype=F32) / HEAD_DIM)
    ang = pos.astype(F32)[:, None] * inv[None, :]
    cos, sin = jnp.cos(ang), jnp.sin(ang)
    return jnp.concatenate([cos, cos], axis=-1), jnp.concatenate([-sin, sin], axis=-1)


def _decay_tables(c):
    log_g = jnp.log1p(-jnp.exp2(-5.0 - jnp.arange(RET_HEADS, dtype=F32)))
    idx = jnp.arange(c, dtype=F32)
    diff = idx[:, None] - idx[None, :]
    dmask = jnp.where(diff >= 0, jnp.exp(log_g[:, None, None] * jnp.maximum(diff, 0.0)), 0.0)
    q_dec = jnp.exp(log_g[:, None] * (idx + 1.0))[..., None]
    k_dec = jnp.exp(log_g[:, None] * (c - 1.0 - idx))[..., None]
    c_dec = jnp.exp(log_g * c)[:, None, None]
    return dmask, q_dec, k_dec, c_dec


def kernel(x_prompt, x_sample, p_prompt, p_sample, state_ret, state_conv, g_attn, w_in, gm_ln_g,
           gm_ln_b, gm_ws, gm_bs, w_o, g_ffn, w_up, conv_w, conv_b, w_down, g_ple, w_ple_gate,
           w_ple_proj, g_final):
    batch, seq, d = x_prompt.shape
    nb = x_sample.shape[0]
    assert x_sample.shape[1] == 1 and g_attn.shape[0] == 1

    w_in_b = w_in[0].astype(BF16)
    w_o_b = w_o[0].astype(BF16)
    w_up_b = w_up[0].astype(BF16)
    w_down_b = w_down[0].astype(BF16)
    w_gate_b = w_ple_gate[0].astype(BF16)
    w_proj_b = w_ple_proj[0].astype(BF16)

    g_attn2, g_ffn2, g_ple2 = g_attn[0][None], g_ffn[0][None], g_ple[0][None]
    g_fin2 = g_final[None]
    lng, lnb = gm_ln_g[0][None], gm_ln_b[0][None]
    ws, bs = gm_ws[0], gm_bs[0]
    cw, cb = conv_w[0], conv_b[0][None]
    n_groups = GM_WIDTH // HEAD_DIM

    xp = x_prompt.reshape(batch * seq, d)
    cos_p, sin_p = _rope_tables(jnp.arange(seq, dtype=jnp.int32))
    dmask, q_dec, k_dec, c_dec = _decay_tables(CHUNK)
    qdec_p = jnp.broadcast_to(q_dec, (RET_HEADS, CHUNK, HEAD_DIM))
    kdec_p = jnp.broadcast_to(k_dec, (RET_HEADS, CHUNK, HEAD_DIM))
    cdec_p = jnp.broadcast_to(c_dec, (RET_HEADS, 1, HEAD_DIM))
    bsf = jnp.broadcast_to(bs.T[:, :, None], (CHUNK, n_groups, HEAD_DIM)).reshape(CHUNK, GM_WIDTH)

    z_p = _norm_matmul(xp, g_attn2, w_in_b, tm=1024, tn=1024, name="prompt_in_proj")
    mix_p, ret_p = _prompt_mixer(z_p, batch, seq, cos_p, sin_p, dmask, qdec_p, kdec_p, cdec_p,
                                 lng, lnb, ws, bsf)
    h1_p, xn2_p = _attn_out(mix_p, w_o_b, xp, g_ffn2, tm=512, name="prompt_attn_out")
    act_p, csg_p, csu_p = _ffn_up(xn2_p, w_up_b, cw, cb, batch=batch, seq=seq, tm=1024, tn=512)
    h2_p = _matmul_residual(act_p, w_down_b, h1_p, tm=1024, tn=512, name="prompt_ffn_down")
    y_p = _ple_final(h2_p, p_prompt[0].reshape(batch * seq, PLE_DIM), g_ple2, w_gate_b, w_proj_b,
                     g_fin2, tm=256, name="prompt_ple_final")

    xs = x_sample.reshape(nb, d)
    cos_s, sin_s = _rope_tables(PAST_LEN + jnp.arange(1, dtype=jnp.int32))
    dmask1, q_dec1, k_dec1, c_dec1 = _decay_tables(1)
    qdec_s = jnp.broadcast_to(q_dec1, (RET_HEADS, 1, HEAD_DIM))
    kdec_s = jnp.broadcast_to(k_dec1, (RET_HEADS, 1, HEAD_DIM))
    cdec_s = jnp.broadcast_to(c_dec1, (RET_HEADS, 1, HEAD_DIM))
    w00 = jnp.broadcast_to(ws[:, 0, 0][:, None], (n_groups, HEAD_DIM)).reshape(1, GM_WIDTH)
    bs0 = jnp.broadcast_to(bs[:, 0][:, None], (n_groups, HEAD_DIM)).reshape(1, GM_WIDTH)

    z_s = _norm_matmul(xs, g_attn2, w_in_b, tm=nb, tn=1024, name="sample_in_proj")
    mix_s, ret_s, vn_s = _sample_mixer(z_s, state_ret[0], cos_s, sin_s, dmask1, qdec_s, kdec_s,
                                       cdec_s, lng, lnb, w00, bs0, tb=8)
    h1_s, xn2_s = _attn_out(mix_s, w_o_b, xs, g_ffn2, tm=nb, name="sample_attn_out")
    conv_state2d = state_conv[0].reshape(nb, 2 * 2 * D_FF)
    a_s, ac_s = _sample_ffn_up(xn2_s, w_up_b, conv_state2d, cw, cb, tn=512)
    h2_s = _sample_ffn_down(ac_s, w_down_b, h1_s, tk=512)
    y_s = _ple_final(h2_s, p_sample[0].reshape(nb, PLE_DIM), g_ple2, w_gate_b, w_proj_b, g_fin2,
                     tm=nb, name="sample_ple_final")

    conv_p = jnp.concatenate([csg_p, csu_p], axis=-1)[None]
    conv_s = jnp.stack([state_conv[0][:, 1, :], a_s], axis=1)[None]
    return (y_p.reshape(batch, seq, d), y_s.reshape(nb, 1, d), ret_p[None], conv_p,
            ret_s[None], conv_s, vn_s.reshape(1, nb, 1, GM_WIDTH))
```

```python
import functools
import math

import jax
import jax.numpy as jnp
from jax import lax
from jax.experimental import pallas as pl
from jax.experimental.pallas import tpu as pltpu

F32 = jnp.float32
BF16 = jnp.bfloat16

D_MODEL = 2048
RET_HEADS = 8
HEAD_DIM = 128
CHUNK = 128
RET_QK = RET_HEADS * HEAD_DIM
GM_WIDTH = 1024
IN_COLS = 6144
D_FF = 5632
PLE_DIM = 256
ROPE_THETA = 10000.0
PAST_LEN = 16384
EPS = 1e-6
K_SCALE = HEAD_DIM ** -0.5

OFF_Q, OFF_K, OFF_V, OFF_G, OFF_U, OFF_VG = 0, 1024, 2048, 3072, 4096, 5120

VMEM_LIMIT = 56 * 1024 * 1024
LANES = 128


def _params(semantics):
    return pltpu.CompilerParams(dimension_semantics=semantics, vmem_limit_bytes=VMEM_LIMIT)


def _rms_rows(x, g):
    ms = jnp.mean(x * x, axis=-1, keepdims=True)
    return x * lax.rsqrt(ms + EPS) * g


def _gelu(x):
    return jax.nn.gelu(x)


def _silu(x):
    return x * jax.nn.sigmoid(x)


def _norm_matmul_kernel(x_ref, g_ref, w_ref, o_ref, xn_ref, *, row_chunk):
    @pl.when(pl.program_id(1) == 0)
    def _():
        def body(r, carry):
            rows = pl.ds(pl.multiple_of(r * row_chunk, row_chunk), row_chunk)
            xn_ref[rows, :] = _rms_rows(x_ref[rows, :], g_ref[...]).astype(BF16)
            return carry
        lax.fori_loop(0, x_ref.shape[0] // row_chunk, body, 0)

    o_ref[...] = jnp.dot(xn_ref[...], w_ref[...], preferred_element_type=F32)


def _norm_matmul(x, g, w, *, tm, tn, name):
    m, k = x.shape
    n = w.shape[1]
    row_chunk = min(tm, 32)
    return pl.pallas_call(
        functools.partial(_norm_matmul_kernel, row_chunk=row_chunk),
        grid=(m // tm, n // tn),
        in_specs=[
            pl.BlockSpec((tm, k), lambda i, j: (i, 0)),
            pl.BlockSpec((1, k), lambda i, j: (0, 0)),
            pl.BlockSpec((k, tn), lambda i, j: (0, j)),
        ],
        out_specs=pl.BlockSpec((tm, tn), lambda i, j: (i, j)),
        out_shape=jax.ShapeDtypeStruct((m, n), F32),
        scratch_shapes=[pltpu.VMEM((tm, k), BF16)],
        compiler_params=_params(("arbitrary", "arbitrary")),
        name=name,
    )(x, g, w)


def _rope(x, cosf, sinf):
    return x * cosf + pltpu.roll(x, HEAD_DIM // 2, 1) * sinf


def _layernorm_rows(x, g, b):
    mu = jnp.mean(x, axis=-1, keepdims=True)
    xc = x - mu
    var = jnp.mean(xc * xc, axis=-1, keepdims=True)
    return xc * lax.rsqrt(var + EPS) * g + b


def _prompt_mixer_kernel(z_ref, cos_ref, sin_ref, dmask_ref, qdec_ref, kdec_ref, cdec_ref,
                         lng_ref, lnb_ref, ws_ref, bsf_ref, mix_ref, s_ref):
    @pl.when(pl.program_id(1) == 0)
    def _():
        s_ref[...] = jnp.zeros_like(s_ref)

    cosf = cos_ref[...]
    sinf = sin_ref[...]
    for h in range(RET_HEADS):
        cols = slice(h * HEAD_DIM, (h + 1) * HEAD_DIM)
        q = _rope(z_ref[:, OFF_Q + h * HEAD_DIM:OFF_Q + (h + 1) * HEAD_DIM], cosf, sinf)
        k = _rope(z_ref[:, OFF_K + h * HEAD_DIM:OFF_K + (h + 1) * HEAD_DIM], cosf, sinf) * K_SCALE
        v = z_ref[:, OFF_V + h * HEAD_DIM:OFF_V + (h + 1) * HEAD_DIM].astype(BF16)
        g = z_ref[:, OFF_G + h * HEAD_DIM:OFF_G + (h + 1) * HEAD_DIM]
        s_old = s_ref[0, h]
        sc = lax.dot_general(q.astype(BF16), k.astype(BF16), (((1,), (1,)), ((), ())),
                             preferred_element_type=F32) * dmask_ref[h]
        o = (jnp.dot(sc.astype(BF16), v, preferred_element_type=F32)
             + jnp.dot((q * qdec_ref[h]).astype(BF16), s_old.astype(BF16),
                       preferred_element_type=F32))
        s_ref[0, h] = s_old * cdec_ref[h] + lax.dot_general(
            (k * kdec_ref[h]).astype(BF16), v, (((0,), (0,)), ((), ())),
            preferred_element_type=F32)
        o = o * lax.rsqrt(jnp.mean(o * o, axis=-1, keepdims=True) + EPS)
        mix_ref[:, cols] = (o * _silu(g)).astype(mix_ref.dtype)

    vn = _layernorm_rows(_gelu(z_ref[:, OFF_VG:OFF_VG + GM_WIDTH]), lng_ref[...], lnb_ref[...])
    row = lax.broadcasted_iota(jnp.int32, (CHUNK, CHUNK), 0)
    col = lax.broadcasted_iota(jnp.int32, (CHUNK, CHUNK), 1)
    causal = row >= col
    for grp in range(GM_WIDTH // HEAD_DIM):
        cols = slice(grp * HEAD_DIM, (grp + 1) * HEAD_DIM)
        wm = jnp.where(causal, ws_ref[grp], 0.0).astype(BF16)
        sp = jnp.dot(wm, vn[:, cols].astype(BF16), preferred_element_type=F32) + bsf_ref[:, cols]
        u = _gelu(z_ref[:, OFF_U + grp * HEAD_DIM:OFF_U + (grp + 1) * HEAD_DIM])
        mix_ref[:, RET_QK + grp * HEAD_DIM:RET_QK + (grp + 1) * HEAD_DIM] = (u * sp).astype(mix_ref.dtype)


def _prompt_mixer(z, batch, seq, cosf, sinf, dmask, qdec, kdec, cdec, lng, lnb, ws, bsf):
    n_chunks = seq // CHUNK
    const3 = lambda b, c: (0, 0, 0)
    const2 = lambda b, c: (0, 0)
    return pl.pallas_call(
        _prompt_mixer_kernel,
        grid=(batch, n_chunks),
        in_specs=[
            pl.BlockSpec((CHUNK, IN_COLS), lambda b, c: (b * n_chunks + c, 0)),
            pl.BlockSpec((CHUNK, HEAD_DIM), lambda b, c: (c, 0)),
            pl.BlockSpec((CHUNK, HEAD_DIM), lambda b, c: (c, 0)),
            pl.BlockSpec((RET_HEADS, CHUNK, CHUNK), const3),
            pl.BlockSpec((RET_HEADS, CHUNK, HEAD_DIM), const3),
            pl.BlockSpec((RET_HEADS, CHUNK, HEAD_DIM), const3),
            pl.BlockSpec((RET_HEADS, 1, HEAD_DIM), const3),
            pl.BlockSpec((1, GM_WIDTH), const2),
            pl.BlockSpec((1, GM_WIDTH), const2),
            pl.BlockSpec((GM_WIDTH // HEAD_DIM, CHUNK, CHUNK), const3),
            pl.BlockSpec((CHUNK, GM_WIDTH), const2),
        ],
        out_specs=[
            pl.BlockSpec((CHUNK, D_MODEL), lambda b, c: (b * n_chunks + c, 0)),
            pl.BlockSpec((1, RET_HEADS, HEAD_DIM, HEAD_DIM), lambda b, c: (b, 0, 0, 0)),
        ],
        out_shape=[
            jax.ShapeDtypeStruct((batch * seq, D_MODEL), BF16),
            jax.ShapeDtypeStruct((batch, RET_HEADS, HEAD_DIM, HEAD_DIM), F32),
        ],
        compiler_params=_params(("arbitrary", "arbitrary")),
        name="prompt_mixer",
    )(z, cosf, sinf, dmask, qdec, kdec, cdec, lng, lnb, ws, bsf)


def _sample_mixer_kernel(z_ref, s_ref, cos_ref, sin_ref, dmask_ref, qdec_ref, kdec_ref, cdec_ref,
                         lng_ref, lnb_ref, w00_ref, bs0_ref, mix_ref, so_ref, vn_ref, *, tb):
    cosf = cos_ref[...]
    sinf = sin_ref[...]
    row = lax.broadcasted_iota(jnp.int32, (HEAD_DIM, HEAD_DIM), 0)
    col = lax.broadcasted_iota(jnp.int32, (HEAD_DIM, HEAD_DIM), 1)
    eye = row == col
    ones = jnp.ones((HEAD_DIM, HEAD_DIM), BF16)

    def lane_broadcast_columns(rows):
        diag = jnp.concatenate(
            [jnp.where(eye, jnp.broadcast_to(rows[i:i + 1, :], (HEAD_DIM, HEAD_DIM)), 0.0)
             for i in range(rows.shape[0])], axis=0)
        hi = diag.astype(BF16)
        lo = (diag - hi.astype(F32)).astype(BF16)
        return (jnp.dot(hi, ones, preferred_element_type=F32)
                + jnp.dot(lo, ones, preferred_element_type=F32))

    for h in range(RET_HEADS):
        cols = slice(h * HEAD_DIM, (h + 1) * HEAD_DIM)
        q = _rope(z_ref[:, OFF_Q + h * HEAD_DIM:OFF_Q + (h + 1) * HEAD_DIM], cosf, sinf)
        k = _rope(z_ref[:, OFF_K + h * HEAD_DIM:OFF_K + (h + 1) * HEAD_DIM], cosf, sinf) * K_SCALE
        v = z_ref[:, OFF_V + h * HEAD_DIM:OFF_V + (h + 1) * HEAD_DIM]
        g = z_ref[:, OFF_G + h * HEAD_DIM:OFF_G + (h + 1) * HEAD_DIM]
        qdec = qdec_ref[h]
        kdec = kdec_ref[h]
        cdec = cdec_ref[h]
        sc = jnp.sum(q * k, axis=-1, keepdims=True) * dmask_ref[h]
        qk_cols = lane_broadcast_columns(jnp.concatenate([q * qdec, k * kdec], axis=0))
        o_rows = []
        for b in range(tb):
            s_old = s_ref[b, h]
            q_col = qk_cols[b * HEAD_DIM:(b + 1) * HEAD_DIM, :]
            k_col = qk_cols[(tb + b) * HEAD_DIM:(tb + b + 1) * HEAD_DIM, :]
            v_row = v[b:b + 1, :]
            o_rows.append(jnp.sum(q_col * s_old, axis=0, keepdims=True))
            so_ref[b, h] = s_old * cdec + k_col * v_row
        o = sc * v + jnp.concatenate(o_rows, axis=0)
        o = o * lax.rsqrt(jnp.mean(o * o, axis=-1, keepdims=True) + EPS)
        mix_ref[:, cols] = o * _silu(g)

    vn = _layernorm_rows(_gelu(z_ref[:, OFF_VG:OFF_VG + GM_WIDTH]), lng_ref[...], lnb_ref[...])
    vn_ref[...] = vn
    u = _gelu(z_ref[:, OFF_U:OFF_U + GM_WIDTH])
    mix_ref[:, RET_QK:RET_QK + GM_WIDTH] = u * (w00_ref[...] * vn + bs0_ref[...])


def _sample_mixer(z, state, cosf, sinf, dmask, qdec, kdec, cdec, lng, lnb, w00, bs0, *, tb):
    nb = z.shape[0]
    const3 = lambda i: (0, 0, 0)
    const2 = lambda i: (0, 0)
    state_spec = pl.BlockSpec((tb, RET_HEADS, HEAD_DIM, HEAD_DIM), lambda i: (i, 0, 0, 0))
    return pl.pallas_call(
        functools.partial(_sample_mixer_kernel, tb=tb),
        grid=(nb // tb,),
        in_specs=[
            pl.BlockSpec((tb, IN_COLS), lambda i: (i, 0)),
            state_spec,
            pl.BlockSpec((1, HEAD_DIM), const2),
            pl.BlockSpec((1, HEAD_DIM), const2),
            pl.BlockSpec((RET_HEADS, 1, 1), const3),
            pl.BlockSpec((RET_HEADS, 1, HEAD_DIM), const3),
            pl.BlockSpec((RET_HEADS, 1, HEAD_DIM), const3),
            pl.BlockSpec((RET_HEADS, 1, HEAD_DIM), const3),
            pl.BlockSpec((1, GM_WIDTH), const2),
            pl.BlockSpec((1, GM_WIDTH), const2),
            pl.BlockSpec((1, GM_WIDTH), const2),
            pl.BlockSpec((1, GM_WIDTH), const2),
        ],
        out_specs=[
            pl.BlockSpec((tb, D_MODEL), lambda i: (i, 0)),
            state_spec,
            pl.BlockSpec((tb, GM_WIDTH), lambda i: (i, 0)),
        ],
        out_shape=[
            jax.ShapeDtypeStruct((nb, D_MODEL), F32),
            jax.ShapeDtypeStruct(state.shape, F32),
            jax.ShapeDtypeStruct((nb, GM_WIDTH), F32),
        ],
        compiler_params=_params(("arbitrary",)),
        name="sample_mixer",
    )(z, state, cosf, sinf, dmask, qdec, kdec, cdec, lng, lnb, w00, bs0)


def _attn_out_kernel(mix_ref, w_ref, h_ref, g_ref, h1_ref, xn_ref, *, row_block, row_chunk):
    tm = h_ref.shape[0]
    for r_lo in range(0, tm, row_block):
        rows = slice(r_lo, r_lo + row_block)
        h1_ref[rows, :] = h_ref[rows, :] + jnp.dot(mix_ref[rows, :].astype(BF16), w_ref[...],
                                                   preferred_element_type=F32)
        for r0 in range(r_lo, r_lo + row_block, row_chunk):
            sub = slice(r0, r0 + row_chunk)
            xn_ref[sub, :] = _rms_rows(h1_ref[sub, :], g_ref[...]).astype(BF16)


def _attn_out(mix, w, h, g, *, tm, name):
    m, d = h.shape
    return pl.pallas_call(
        functools.partial(_attn_out_kernel, row_block=min(tm, 256), row_chunk=min(tm, 32)),
        grid=(m // tm,),
        in_specs=[
            pl.BlockSpec((tm, d), lambda i: (i, 0)),
            pl.BlockSpec((d, d), lambda i: (0, 0)),
            pl.BlockSpec((tm, d), lambda i: (i, 0)),
            pl.BlockSpec((1, d), lambda i: (0, 0)),
        ],
        out_specs=[
            pl.BlockSpec((tm, d), lambda i: (i, 0)),
            pl.BlockSpec((tm, d), lambda i: (i, 0)),
        ],
        out_shape=[
            jax.ShapeDtypeStruct((m, d), F32),
            jax.ShapeDtypeStruct((m, d), BF16),
        ],
        compiler_params=_params(("arbitrary",)),
        name=name,
    )(mix, w, h, g)


CONV_PAD = 8


def _ffn_up_kernel(xn_ref, wg_ref, wu_ref, cg_ref, cu_ref, act_ref, csg_ref, csu_ref,
                   ag_ref, au_ref, *, tiles_per_seq, row_block, row_chunk):
    i = pl.program_id(1)
    tm, tn = act_ref.shape
    first = (i % tiles_per_seq) == 0

    @pl.when(first)
    def _():
        ag_ref[0:CONV_PAD, :] = jnp.zeros((CONV_PAD, tn), F32)
        au_ref[0:CONV_PAD, :] = jnp.zeros((CONV_PAD, tn), F32)

    @pl.when(jnp.logical_not(first))
    def _():
        ag_ref[0:CONV_PAD, :] = ag_ref[tm:tm + CONV_PAD, :]
        au_ref[0:CONV_PAD, :] = au_ref[tm:tm + CONV_PAD, :]

    def matmuls(r_lo):
        xn = xn_ref[r_lo:r_lo + row_block, :]
        dst = slice(CONV_PAD + r_lo, CONV_PAD + r_lo + row_block)
        ag_ref[dst, :] = jnp.dot(xn, wg_ref[...], preferred_element_type=F32)
        au_ref[dst, :] = jnp.dot(xn, wu_ref[...], preferred_element_type=F32)

    def conv(a_ref, c_ref, r0):
        def window(back):
            lo = r0 + CONV_PAD - back
            return a_ref[lo:lo + row_chunk, :].reshape(row_chunk // 8, 8, tn)
        return window(2) * c_ref[0] + window(1) * c_ref[1] + window(0) * c_ref[2] + c_ref[3]

    def epilogue(r_lo):
        for r0 in range(r_lo, r_lo + row_block, row_chunk):
            act = (_silu(conv(ag_ref, cg_ref, r0)) * conv(au_ref, cu_ref, r0)).reshape(row_chunk, tn)
            act_ref[r0:r0 + row_chunk, :] = act.astype(act_ref.dtype)

    matmuls(0)
    for r_lo in range(row_block, tm, row_block):
        matmuls(r_lo)
        epilogue(r_lo - row_block)
    epilogue(tm - row_block)

    @pl.when((i % tiles_per_seq) == tiles_per_seq - 1)
    def _():
        csg_ref[0] = ag_ref[CONV_PAD + tm - 2:CONV_PAD + tm, :]
        csu_ref[0] = au_ref[CONV_PAD + tm - 2:CONV_PAD + tm, :]


def _ffn_up(xn, w_up, conv_taps, *, batch, seq, tm, tn):
    m, d = xn.shape
    n_col = D_FF // tn
    tiles_per_seq = seq // tm
    return pl.pallas_call(
        functools.partial(_ffn_up_kernel, tiles_per_seq=tiles_per_seq, row_block=256,
                          row_chunk=64),
        grid=(n_col, m // tm),
        in_specs=[
            pl.BlockSpec((tm, d), lambda j, i: (i, 0)),
            pl.BlockSpec((d, tn), lambda j, i: (0, j)),
            pl.BlockSpec((d, tn), lambda j, i: (0, n_col + j)),
            pl.BlockSpec((4, 8, tn), lambda j, i: (0, 0, j)),
            pl.BlockSpec((4, 8, tn), lambda j, i: (0, 0, n_col + j)),
        ],
        out_specs=[
            pl.BlockSpec((tm, tn), lambda j, i: (i, j)),
            pl.BlockSpec((1, 2, tn), lambda j, i: (i // tiles_per_seq, 0, j)),
            pl.BlockSpec((1, 2, tn), lambda j, i: (i // tiles_per_seq, 0, j)),
        ],
        out_shape=[
            jax.ShapeDtypeStruct((m, D_FF), BF16),
            jax.ShapeDtypeStruct((batch, 2, D_FF), F32),
            jax.ShapeDtypeStruct((batch, 2, D_FF), F32),
        ],
        scratch_shapes=[pltpu.VMEM((tm + CONV_PAD, tn), F32), pltpu.VMEM((tm + CONV_PAD, tn), F32)],
        compiler_params=_params(("arbitrary", "arbitrary")),
        name="prompt_ffn_up",
    )(xn, w_up, w_up, conv_taps, conv_taps)


def _matmul_residual_kernel(a_ref, w_ref, r_ref, o_ref):
    o_ref[...] = r_ref[...] + jnp.dot(a_ref[...], w_ref[...], preferred_element_type=F32)


def _matmul_residual(a, w, res, *, tm, tn, name):
    m, k = a.shape
    n = w.shape[1]
    return pl.pallas_call(
        _matmul_residual_kernel,
        grid=(m // tm, n // tn),
        in_specs=[
            pl.BlockSpec((tm, k), lambda i, j: (i, 0)),
            pl.BlockSpec((k, tn), lambda i, j: (0, j)),
            pl.BlockSpec((tm, tn), lambda i, j: (i, j)),
        ],
        out_specs=pl.BlockSpec((tm, tn), lambda i, j: (i, j)),
        out_shape=jax.ShapeDtypeStruct((m, n), F32),
        compiler_params=_params(("arbitrary", "arbitrary")),
        name=name,
    )(a, w, res)


def _sample_ffn_up_kernel(xn_ref, w_ref, cb0_ref, cb1_ref, cw_ref, cbias_ref, a_ref, ac_ref):
    a = jnp.dot(xn_ref[...], w_ref[...], preferred_element_type=F32)
    a_ref[...] = a
    ac_ref[...] = (cb0_ref[...] * cw_ref[0:1, :] + cb1_ref[...] * cw_ref[1:2, :]
                   + a * cw_ref[2:3, :] + cbias_ref[...])


def _sample_ffn_up(xn, w_up, conv_state2d, conv_w, conv_b, *, tn):
    m, d = xn.shape
    n = w_up.shape[1]
    n_col = n // tn
    return pl.pallas_call(
        _sample_ffn_up_kernel,
        grid=(n_col,),
        in_specs=[
            pl.BlockSpec((m, d), lambda j: (0, 0)),
            pl.BlockSpec((d, tn), lambda j: (0, j)),
            pl.BlockSpec((m, tn), lambda j: (0, j)),
            pl.BlockSpec((m, tn), lambda j: (0, n_col + j)),
            pl.BlockSpec((3, tn), lambda j: (0, j)),
            pl.BlockSpec((1, tn), lambda j: (0, j)),
        ],
        out_specs=[
            pl.BlockSpec((m, tn), lambda j: (0, j)),
            pl.BlockSpec((m, tn), lambda j: (0, j)),
        ],
        out_shape=[
            jax.ShapeDtypeStruct((m, n), F32),
            jax.ShapeDtypeStruct((m, n), F32),
        ],
        compiler_params=_params(("arbitrary",)),
        name="sample_ffn_up",
    )(xn, w_up, conv_state2d, conv_state2d, conv_w, conv_b)


def _sample_ffn_down_kernel(gate_ref, up_ref, w_ref, h_ref, o_ref):
    @pl.when(pl.program_id(0) == 0)
    def _():
        o_ref[...] = h_ref[...]

    act = (_silu(gate_ref[...]) * up_ref[...]).astype(BF16)
    o_ref[...] += jnp.dot(act, w_ref[...], preferred_element_type=F32)


def _sample_ffn_down(ac, w_down, h1, *, tk):
    m, d = h1.shape
    n_k = D_FF // tk
    return pl.pallas_call(
        _sample_ffn_down_kernel,
        grid=(n_k,),
        in_specs=[
            pl.BlockSpec((m, tk), lambda k: (0, k)),
            pl.BlockSpec((m, tk), lambda k: (0, n_k + k)),
            pl.BlockSpec((tk, d), lambda k: (k, 0)),
            pl.BlockSpec((m, d), lambda k: (0, 0)),
        ],
        out_specs=pl.BlockSpec((m, d), lambda k: (0, 0)),
        out_shape=jax.ShapeDtypeStruct((m, d), F32),
        compiler_params=_params(("arbitrary",)),
        name="sample_ffn_down",
    )(ac, ac, w_down, h1)


def _ple_final_kernel(h_ref, p_ref, gple_ref, wg_ref, wp_ref, gfin_ref, y_ref, xn_ref, *,
                      row_block, row_chunk):
    tm = h_ref.shape[0]
    for r_lo in range(0, tm, row_block):
        rows = slice(r_lo, r_lo + row_block)
        for r0 in range(r_lo, r_lo + row_block, row_chunk):
            sub = slice(r0, r0 + row_chunk)
            xn_ref[sub, :] = _rms_rows(h_ref[sub, :], gple_ref[...]).astype(BF16)
        gate = jax.nn.sigmoid(jnp.dot(xn_ref[rows, :], wg_ref[...], preferred_element_type=F32))
        proj = jnp.dot(p_ref[rows, :].astype(BF16), wp_ref[...], preferred_element_type=F32)
        y_ref[rows, :] = h_ref[rows, :] + proj * gate
        for r0 in range(r_lo, r_lo + row_block, row_chunk):
            sub = slice(r0, r0 + row_chunk)
            y_ref[sub, :] = _rms_rows(y_ref[sub, :], gfin_ref[...])


def _ple_final(h, p, g_ple, w_gate, w_proj, g_final, *, tm, name):
    m, d = h.shape
    pd = p.shape[1]
    return pl.pallas_call(
        functools.partial(_ple_final_kernel, row_block=min(tm, 256), row_chunk=min(tm, 32)),
        grid=(m // tm,),
        in_specs=[
            pl.BlockSpec((tm, d), lambda i: (i, 0)),
            pl.BlockSpec((tm, pd), lambda i: (i, 0)),
            pl.BlockSpec((1, d), lambda i: (0, 0)),
            pl.BlockSpec((d, d), lambda i: (0, 0)),
            pl.BlockSpec((pd, d), lambda i: (0, 0)),
            pl.BlockSpec((1, d), lambda i: (0, 0)),
        ],
        out_specs=pl.BlockSpec((tm, d), lambda i: (i, 0)),
        out_shape=jax.ShapeDtypeStruct((m, d), F32),
        scratch_shapes=[pltpu.VMEM((tm, d), BF16)],
        compiler_params=_params(("arbitrary",)),
        name=name,
    )(h, p, g_ple, w_gate, w_proj, g_final)


def _rope_tables(pos):
    half = HEAD_DIM // 2
    inv = ROPE_THETA ** (-jnp.arange(0, HEAD_DIM, 2, dtype=F32) / HEAD_DIM)
    ang = pos.astype(F32)[:, None] * inv[None, :]
    cos, sin = jnp.cos(ang), jnp.sin(ang)
    return jnp.concatenate([cos, cos], axis=-1), jnp.concatenate([-sin, sin], axis=-1)


def _decay_tables(c):
    log_g = jnp.log1p(-jnp.exp2(-5.0 - jnp.arange(RET_HEADS, dtype=F32)))
    idx = jnp.arange(c, dtype=F32)
    diff = idx[:, None] - idx[None, :]
    dmask = jnp.where(diff >= 0, jnp.exp(log_g[:, None, None] * jnp.maximum(diff, 0.0)), 0.0)
    q_dec = jnp.exp(log_g[:, None] * (idx + 1.0))[..., None]
    k_dec = jnp.exp(log_g[:, None] * (c - 1.0 - idx))[..., None]
    c_dec = jnp.exp(log_g * c)[:, None, None]
    return dmask, q_dec, k_dec, c_dec


def kernel(x_prompt, x_sample, p_prompt, p_sample, state_ret, state_conv, g_attn, w_in, gm_ln_g,
           gm_ln_b, gm_ws, gm_bs, w_o, g_ffn, w_up, conv_w, conv_b, w_down, g_ple, w_ple_gate,
           w_ple_proj, g_final):
    batch, seq, d = x_prompt.shape
    nb = x_sample.shape[0]
    assert x_sample.shape[1] == 1 and g_attn.shape[0] == 1

    w_in_b = w_in[0].astype(BF16)
    w_o_b = w_o[0].astype(BF16)
    w_up_b = w_up[0].astype(BF16)
    w_down_b = w_down[0].astype(BF16)
    w_gate_b = w_ple_gate[0].astype(BF16)
    w_proj_b = w_ple_proj[0].astype(BF16)

    g_attn2, g_ffn2, g_ple2 = g_attn[0][None], g_ffn[0][None], g_ple[0][None]
    g_fin2 = g_final[None]
    lng, lnb = gm_ln_g[0][None], gm_ln_b[0][None]
    ws, bs = gm_ws[0], gm_bs[0]
    cw, cb = conv_w[0], conv_b[0][None]
    n_groups = GM_WIDTH // HEAD_DIM

    xp = x_prompt.reshape(batch * seq, d)
    cos_p, sin_p = _rope_tables(jnp.arange(seq, dtype=jnp.int32))
    dmask, q_dec, k_dec, c_dec = _decay_tables(CHUNK)
    qdec_p = jnp.broadcast_to(q_dec, (RET_HEADS, CHUNK, HEAD_DIM))
    kdec_p = jnp.broadcast_to(k_dec, (RET_HEADS, CHUNK, HEAD_DIM))
    cdec_p = jnp.broadcast_to(c_dec, (RET_HEADS, 1, HEAD_DIM))
    bsf = jnp.broadcast_to(bs.T[:, :, None], (CHUNK, n_groups, HEAD_DIM)).reshape(CHUNK, GM_WIDTH)

    z_p = _norm_matmul(xp, g_attn2, w_in_b, tm=1024, tn=1024, name="prompt_in_proj")
    mix_p, ret_p = _prompt_mixer(z_p, batch, seq, cos_p, sin_p, dmask, qdec_p, kdec_p, cdec_p,
                                 lng, lnb, ws, bsf)
    h1_p, xn2_p = _attn_out(mix_p, w_o_b, xp, g_ffn2, tm=512, name="prompt_attn_out")
    conv_taps = jnp.broadcast_to(jnp.concatenate([cw, cb], axis=0)[:, None, :], (4, 8, 2 * D_FF))
    act_p, csg_p, csu_p = _ffn_up(xn2_p, w_up_b, conv_taps, batch=batch, seq=seq, tm=1024, tn=512)
    h2_p = _matmul_residual(act_p, w_down_b, h1_p, tm=1024, tn=512, name="prompt_ffn_down")
    y_p = _ple_final(h2_p, p_prompt[0].reshape(batch * seq, PLE_DIM), g_ple2, w_gate_b, w_proj_b,
                     g_fin2, tm=256, name="prompt_ple_final")

    xs = x_sample.reshape(nb, d)
    cos_s, sin_s = _rope_tables(PAST_LEN + jnp.arange(1, dtype=jnp.int32))
    dmask1, q_dec1, k_dec1, c_dec1 = _decay_tables(1)
    qdec_s = jnp.broadcast_to(q_dec1, (RET_HEADS, 1, HEAD_DIM))
    kdec_s = jnp.broadcast_to(k_dec1, (RET_HEADS, 1, HEAD_DIM))
    cdec_s = jnp.broadcast_to(c_dec1, (RET_HEADS, 1, HEAD_DIM))
    w00 = jnp.broadcast_to(ws[:, 0, 0][:, None], (n_groups, HEAD_DIM)).reshape(1, GM_WIDTH)
    bs0 = jnp.broadcast_to(bs[:, 0][:, None], (n_groups, HEAD_DIM)).reshape(1, GM_WIDTH)

    z_s = _norm_matmul(xs, g_attn2, w_in_b, tm=nb, tn=1024, name="sample_in_proj")
    mix_s, ret_s, vn_s = _sample_mixer(z_s, state_ret[0], cos_s, sin_s, dmask1, qdec_s, kdec_s,
                                       cdec_s, lng, lnb, w00, bs0, tb=8)
    h1_s, xn2_s = _attn_out(mix_s, w_o_b, xs, g_ffn2, tm=nb, name="sample_attn_out")
    conv_state2d = state_conv[0].reshape(nb, 2 * 2 * D_FF)
    a_s, ac_s = _sample_ffn_up(xn2_s, w_up_b, conv_state2d, cw, cb, tn=512)
    h2_s = _sample_ffn_down(ac_s, w_down_b, h1_s, tk=512)
    y_s = _ple_final(h2_s, p_sample[0].reshape(nb, PLE_DIM), g_ple2, w_gate_b, w_proj_b, g_fin2,
                     tm=nb, name="sample_ple_final")

    conv_p = jnp.concatenate([csg_p, csu_p], axis=-1)[None]
    conv_s = jnp.stack([state_conv[0][:, 1, :], a_s], axis=1)[None]
    return (y_p.reshape(batch, seq, d), y_s.reshape(nb, 1, d), ret_p[None], conv_p,
            ret_s[None], conv_s, vn_s.reshape(1, nb, 1, GM_WIDTH))
```

```python
import functools
import math

import jax
import jax.numpy as jnp
from jax import lax
from jax.experimental import pallas as pl
from jax.experimental.pallas import tpu as pltpu

F32 = jnp.float32
BF16 = jnp.bfloat16

D_MODEL = 2048
RET_HEADS = 8
HEAD_DIM = 128
CHUNK = 128
RET_QK = RET_HEADS * HEAD_DIM
GM_WIDTH = 1024
IN_COLS = 6144
D_FF = 5632
PLE_DIM = 256
ROPE_THETA = 10000.0
PAST_LEN = 16384
EPS = 1e-6
K_SCALE = HEAD_DIM ** -0.5

OFF_Q, OFF_K, OFF_V, OFF_G, OFF_U, OFF_VG = 0, 1024, 2048, 3072, 4096, 5120

VMEM_LIMIT = 56 * 1024 * 1024
LANES = 128
WEIGHT_CAST_ROWS = 256


def _params(semantics):
    return pltpu.CompilerParams(dimension_semantics=semantics, vmem_limit_bytes=VMEM_LIMIT)


def _rms_rows(x, g):
    ms = jnp.mean(x * x, axis=-1, keepdims=True)
    return x * lax.rsqrt(ms + EPS) * g


def _gelu(x):
    return jax.nn.gelu(x)


def _silu(x):
    return x * jax.nn.sigmoid(x)


def _norm_matmul_kernel(x_ref, g_ref, w_ref, o_ref, xn_ref, *, row_chunk):
    @pl.when(pl.program_id(1) == 0)
    def _():
        def body(r, carry):
            rows = pl.ds(pl.multiple_of(r * row_chunk, row_chunk), row_chunk)
            xn_ref[rows, :] = _rms_rows(x_ref[rows, :], g_ref[...]).astype(BF16)
            return carry
        lax.fori_loop(0, x_ref.shape[0] // row_chunk, body, 0)

    o_ref[...] = jnp.dot(xn_ref[...], w_ref[...], preferred_element_type=F32)


def _norm_matmul(x, g, w, *, tm, tn, name):
    m, k = x.shape
    n = w.shape[1]
    row_chunk = min(tm, 32)
    return pl.pallas_call(
        functools.partial(_norm_matmul_kernel, row_chunk=row_chunk),
        grid=(m // tm, n // tn),
        in_specs=[
            pl.BlockSpec((tm, k), lambda i, j: (i, 0)),
            pl.BlockSpec((1, k), lambda i, j: (0, 0)),
            pl.BlockSpec((k, tn), lambda i, j: (0, j)),
        ],
        out_specs=pl.BlockSpec((tm, tn), lambda i, j: (i, j)),
        out_shape=jax.ShapeDtypeStruct((m, n), F32),
        scratch_shapes=[pltpu.VMEM((tm, k), BF16)],
        compiler_params=_params(("arbitrary", "arbitrary")),
        name=name,
    )(x, g, w)


def _rope(x, cosf, sinf):
    return x * cosf + pltpu.roll(x, HEAD_DIM // 2, 1) * sinf


def _layernorm_rows(x, g, b):
    mu = jnp.mean(x, axis=-1, keepdims=True)
    xc = x - mu
    var = jnp.mean(xc * xc, axis=-1, keepdims=True)
    return xc * lax.rsqrt(var + EPS) * g + b


def _prompt_mixer_kernel(z_ref, cos_ref, sin_ref, dmask_ref, qdec_ref, kdec_ref, cdec_ref,
                         lng_ref, lnb_ref, ws_ref, bsf_ref, mix_ref, s_ref):
    @pl.when(pl.program_id(1) == 0)
    def _():
        s_ref[...] = jnp.zeros_like(s_ref)

    cosf = cos_ref[...]
    sinf = sin_ref[...]
    for h in range(RET_HEADS):
        cols = slice(h * HEAD_DIM, (h + 1) * HEAD_DIM)
        q = _rope(z_ref[:, OFF_Q + h * HEAD_DIM:OFF_Q + (h + 1) * HEAD_DIM], cosf, sinf)
        k = _rope(z_ref[:, OFF_K + h * HEAD_DIM:OFF_K + (h + 1) * HEAD_DIM], cosf, sinf) * K_SCALE
        v = z_ref[:, OFF_V + h * HEAD_DIM:OFF_V + (h + 1) * HEAD_DIM].astype(BF16)
        g = z_ref[:, OFF_G + h * HEAD_DIM:OFF_G + (h + 1) * HEAD_DIM]
        s_old = s_ref[0, h]
        sc = lax.dot_general(q.astype(BF16), k.astype(BF16), (((1,), (1,)), ((), ())),
                             preferred_element_type=F32) * dmask_ref[h]
        o = (jnp.dot(sc.astype(BF16), v, preferred_element_type=F32)
             + jnp.dot((q * qdec_ref[h]).astype(BF16), s_old.astype(BF16),
                       preferred_element_type=F32))
        s_ref[0, h] = s_old * cdec_ref[h] + lax.dot_general(
            (k * kdec_ref[h]).astype(BF16), v, (((0,), (0,)), ((), ())),
            preferred_element_type=F32)
        o = o * lax.rsqrt(jnp.mean(o * o, axis=-1, keepdims=True) + EPS)
        mix_ref[:, cols] = (o * _silu(g)).astype(mix_ref.dtype)

    vn = _layernorm_rows(_gelu(z_ref[:, OFF_VG:OFF_VG + GM_WIDTH]), lng_ref[...], lnb_ref[...])
    row = lax.broadcasted_iota(jnp.int32, (CHUNK, CHUNK), 0)
    col = lax.broadcasted_iota(jnp.int32, (CHUNK, CHUNK), 1)
    causal = row >= col
    for grp in range(GM_WIDTH // HEAD_DIM):
        cols = slice(grp * HEAD_DIM, (grp + 1) * HEAD_DIM)
        wm = jnp.where(causal, ws_ref[grp], 0.0).astype(BF16)
        sp = jnp.dot(wm, vn[:, cols].astype(BF16), preferred_element_type=F32) + bsf_ref[:, cols]
        u = _gelu(z_ref[:, OFF_U + grp * HEAD_DIM:OFF_U + (grp + 1) * HEAD_DIM])
        mix_ref[:, RET_QK + grp * HEAD_DIM:RET_QK + (grp + 1) * HEAD_DIM] = (u * sp).astype(mix_ref.dtype)


def _prompt_mixer(z, batch, seq, cosf, sinf, dmask, qdec, kdec, cdec, lng, lnb, ws, bsf):
    n_chunks = seq // CHUNK
    const3 = lambda b, c: (0, 0, 0)
    const2 = lambda b, c: (0, 0)
    return pl.pallas_call(
        _prompt_mixer_kernel,
        grid=(batch, n_chunks),
        in_specs=[
            pl.BlockSpec((CHUNK, IN_COLS), lambda b, c: (b * n_chunks + c, 0)),
            pl.BlockSpec((CHUNK, HEAD_DIM), lambda b, c: (c, 0)),
            pl.BlockSpec((CHUNK, HEAD_DIM), lambda b, c: (c, 0)),
            pl.BlockSpec((RET_HEADS, CHUNK, CHUNK), const3),
            pl.BlockSpec((RET_HEADS, CHUNK, HEAD_DIM), const3),
            pl.BlockSpec((RET_HEADS, CHUNK, HEAD_DIM), const3),
            pl.BlockSpec((RET_HEADS, 1, HEAD_DIM), const3),
            pl.BlockSpec((1, GM_WIDTH), const2),
            pl.BlockSpec((1, GM_WIDTH), const2),
            pl.BlockSpec((GM_WIDTH // HEAD_DIM, CHUNK, CHUNK), const3),
            pl.BlockSpec((CHUNK, GM_WIDTH), const2),
        ],
        out_specs=[
            pl.BlockSpec((CHUNK, D_MODEL), lambda b, c: (b * n_chunks + c, 0)),
            pl.BlockSpec((1, RET_HEADS, HEAD_DIM, HEAD_DIM), lambda b, c: (b, 0, 0, 0)),
        ],
        out_shape=[
            jax.ShapeDtypeStruct((batch * seq, D_MODEL), BF16),
            jax.ShapeDtypeStruct((batch, RET_HEADS, HEAD_DIM, HEAD_DIM), F32),
        ],
        compiler_params=_params(("arbitrary", "arbitrary")),
        name="prompt_mixer",
    )(z, cosf, sinf, dmask, qdec, kdec, cdec, lng, lnb, ws, bsf)


def _sample_mixer_kernel(z_ref, s_ref, cos_ref, sin_ref, dmask_ref, qdec_ref, kdec_ref, cdec_ref,
                         lng_ref, lnb_ref, w00_ref, bs0_ref, mix_ref, so_ref, vn_ref, *, tb):
    cosf = cos_ref[...]
    sinf = sin_ref[...]
    row = lax.broadcasted_iota(jnp.int32, (HEAD_DIM, HEAD_DIM), 0)
    col = lax.broadcasted_iota(jnp.int32, (HEAD_DIM, HEAD_DIM), 1)
    eye = row == col
    ones = jnp.ones((HEAD_DIM, HEAD_DIM), BF16)

    def lane_broadcast_columns(rows):
        diag = jnp.concatenate(
            [jnp.where(eye, jnp.broadcast_to(rows[i:i + 1, :], (HEAD_DIM, HEAD_DIM)), 0.0)
             for i in range(rows.shape[0])], axis=0)
        hi = diag.astype(BF16)
        lo = (diag - hi.astype(F32)).astype(BF16)
        return (jnp.dot(hi, ones, preferred_element_type=F32)
                + jnp.dot(lo, ones, preferred_element_type=F32))

    for h in range(RET_HEADS):
        cols = slice(h * HEAD_DIM, (h + 1) * HEAD_DIM)
        q = _rope(z_ref[:, OFF_Q + h * HEAD_DIM:OFF_Q + (h + 1) * HEAD_DIM], cosf, sinf)
        k = _rope(z_ref[:, OFF_K + h * HEAD_DIM:OFF_K + (h + 1) * HEAD_DIM], cosf, sinf) * K_SCALE
        v = z_ref[:, OFF_V + h * HEAD_DIM:OFF_V + (h + 1) * HEAD_DIM]
        g = z_ref[:, OFF_G + h * HEAD_DIM:OFF_G + (h + 1) * HEAD_DIM]
        qdec = qdec_ref[h]
        kdec = kdec_ref[h]
        cdec = cdec_ref[h]
        sc = jnp.sum(q * k, axis=-1, keepdims=True) * dmask_ref[h]
        qk_cols = lane_broadcast_columns(jnp.concatenate([q * qdec, k * kdec], axis=0))
        o_rows = []
        for b in range(tb):
            s_old = s_ref[b, h]
            q_col = qk_cols[b * HEAD_DIM:(b + 1) * HEAD_DIM, :]
            k_col = qk_cols[(tb + b) * HEAD_DIM:(tb + b + 1) * HEAD_DIM, :]
            v_row = v[b:b + 1, :]
            o_rows.append(jnp.sum(q_col * s_old, axis=0, keepdims=True))
            so_ref[b, h] = s_old * cdec + k_col * v_row
        o = sc * v + jnp.concatenate(o_rows, axis=0)
        o = o * lax.rsqrt(jnp.mean(o * o, axis=-1, keepdims=True) + EPS)
        mix_ref[:, cols] = o * _silu(g)

    vn = _layernorm_rows(_gelu(z_ref[:, OFF_VG:OFF_VG + GM_WIDTH]), lng_ref[...], lnb_ref[...])
    vn_ref[...] = vn
    u = _gelu(z_ref[:, OFF_U:OFF_U + GM_WIDTH])
    mix_ref[:, RET_QK:RET_QK + GM_WIDTH] = u * (w00_ref[...] * vn + bs0_ref[...])


def _sample_mixer(z, state, cosf, sinf, dmask, qdec, kdec, cdec, lng, lnb, w00, bs0, *, tb):
    nb = z.shape[0]
    const3 = lambda i: (0, 0, 0)
    const2 = lambda i: (0, 0)
    state_spec = pl.BlockSpec((tb, RET_HEADS, HEAD_DIM, HEAD_DIM), lambda i: (i, 0, 0, 0))
    return pl.pallas_call(
        functools.partial(_sample_mixer_kernel, tb=tb),
        grid=(nb // tb,),
        in_specs=[
            pl.BlockSpec((tb, IN_COLS), lambda i: (i, 0)),
            state_spec,
            pl.BlockSpec((1, HEAD_DIM), const2),
            pl.BlockSpec((1, HEAD_DIM), const2),
            pl.BlockSpec((RET_HEADS, 1, 1), const3),
            pl.BlockSpec((RET_HEADS, 1, HEAD_DIM), const3),
            pl.BlockSpec((RET_HEADS, 1, HEAD_DIM), const3),
            pl.BlockSpec((RET_HEADS, 1, HEAD_DIM), const3),
            pl.BlockSpec((1, GM_WIDTH), const2),
            pl.BlockSpec((1, GM_WIDTH), const2),
            pl.BlockSpec((1, GM_WIDTH), const2),
            pl.BlockSpec((1, GM_WIDTH), const2),
        ],
        out_specs=[
            pl.BlockSpec((tb, D_MODEL), lambda i: (i, 0)),
            state_spec,
            pl.BlockSpec((tb, GM_WIDTH), lambda i: (i, 0)),
        ],
        out_shape=[
            jax.ShapeDtypeStruct((nb, D_MODEL), F32),
            jax.ShapeDtypeStruct(state.shape, F32),
            jax.ShapeDtypeStruct((nb, GM_WIDTH), F32),
        ],
        compiler_params=_params(("arbitrary",)),
        name="sample_mixer",
    )(z, state, cosf, sinf, dmask, qdec, kdec, cdec, lng, lnb, w00, bs0)


def _attn_out_kernel(mix_ref, w_ref, h_ref, g_ref, h1_ref, xn_ref, *, row_block, row_chunk):
    tm = h_ref.shape[0]
    for r_lo in range(0, tm, row_block):
        rows = slice(r_lo, r_lo + row_block)
        h1_ref[rows, :] = h_ref[rows, :] + jnp.dot(mix_ref[rows, :].astype(BF16), w_ref[...],
                                                   preferred_element_type=F32)
        for r0 in range(r_lo, r_lo + row_block, row_chunk):
            sub = slice(r0, r0 + row_chunk)
            xn_ref[sub, :] = _rms_rows(h1_ref[sub, :], g_ref[...]).astype(BF16)


def _attn_out(mix, w, h, g, *, tm, name):
    m, d = h.shape
    return pl.pallas_call(
        functools.partial(_attn_out_kernel, row_block=min(tm, 256), row_chunk=min(tm, 32)),
        grid=(m // tm,),
        in_specs=[
            pl.BlockSpec((tm, d), lambda i: (i, 0)),
            pl.BlockSpec((d, d), lambda i: (0, 0)),
            pl.BlockSpec((tm, d), lambda i: (i, 0)),
            pl.BlockSpec((1, d), lambda i: (0, 0)),
        ],
        out_specs=[
            pl.BlockSpec((tm, d), lambda i: (i, 0)),
            pl.BlockSpec((tm, d), lambda i: (i, 0)),
        ],
        out_shape=[
            jax.ShapeDtypeStruct((m, d), F32),
            jax.ShapeDtypeStruct((m, d), BF16),
        ],
        compiler_params=_params(("arbitrary",)),
        name=name,
    )(mix, w, h, g)


CONV_PAD = 8


def _ffn_up_kernel(xn_ref, wg32_ref, wu32_ref, cg_ref, cu_ref, xs_ref, s0g_ref, s0u_ref, s1g_ref,
                   s1u_ref, act_ref, csg_ref, csu_ref, asg_ref, asu_ref, acts_ref,
                   ag_ref, au0_ref, au1_ref, sg_ref, wg_ref, wu_ref, *, tiles_per_seq, row_chunk):
    i = pl.program_id(1)
    tm, tn = act_ref.shape
    half = tm // 2
    first = (i % tiles_per_seq) == 0

    @pl.when(first)
    def _():
        ag_ref[0:CONV_PAD, :] = jnp.zeros((CONV_PAD, tn), F32)
        au0_ref[0:CONV_PAD, :] = jnp.zeros((CONV_PAD, tn), F32)

    @pl.when(jnp.logical_not(first))
    def _():
        ag_ref[0:CONV_PAD, :] = ag_ref[tm:tm + CONV_PAD, :]
        au0_ref[0:CONV_PAD, :] = au1_ref[half:half + CONV_PAD, :]

    @pl.when(i == 0)
    def _():
        k_rows = wg_ref.shape[0]
        for k0 in range(0, k_rows, WEIGHT_CAST_ROWS):
            rows = slice(k0, k0 + WEIGHT_CAST_ROWS)
            wg_ref[rows, :] = wg32_ref[rows, :].astype(BF16)
            wu_ref[rows, :] = wu32_ref[rows, :].astype(BF16)
        xs = xs_ref[...]
        nb = xs.shape[0]
        a_g = jnp.dot(xs, wg_ref[...], preferred_element_type=F32)
        a_u = jnp.dot(xs, wu_ref[...], preferred_element_type=F32)
        asg_ref[...] = a_g
        asu_ref[...] = a_u

        def conv_s(s0_ref, s1_ref, a, c_ref):
            t = lambda v: v.reshape(nb // 8, 8, tn)
            return (t(s0_ref[...]) * c_ref[0] + t(s1_ref[...]) * c_ref[1] + t(a) * c_ref[2]
                    + c_ref[3]).reshape(nb, tn)
        acts_ref[...] = (_silu(conv_s(s0g_ref, s1g_ref, a_g, cg_ref))
                         * conv_s(s0u_ref, s1u_ref, a_u, cu_ref)).astype(acts_ref.dtype)

    def conv(a_ref, c_ref, r0):
        def window(back):
            lo = r0 + CONV_PAD - back
            return a_ref[lo:lo + row_chunk, :].reshape(row_chunk // 8, 8, tn)
        return window(2) * c_ref[0] + window(1) * c_ref[1] + window(0) * c_ref[2] + c_ref[3]

    def up_epilogue(au_ref, base):
        for r0 in range(0, half, row_chunk):
            up = conv(au_ref, cu_ref, r0).reshape(row_chunk, tn)
            rows = slice(base + r0, base + r0 + row_chunk)
            act_ref[rows, :] = (sg_ref[rows, :] * up).astype(act_ref.dtype)

    ag_ref[CONV_PAD:CONV_PAD + tm, :] = jnp.dot(xn_ref[...], wg_ref[...],
                                                preferred_element_type=F32)
    au0_ref[CONV_PAD:CONV_PAD + half, :] = jnp.dot(xn_ref[0:half, :], wu_ref[...],
                                                   preferred_element_type=F32)
    au1_ref[0:CONV_PAD, :] = au0_ref[half:half + CONV_PAD, :]
    for r0 in range(0, tm, row_chunk):
        sg_ref[r0:r0 + row_chunk, :] = _silu(conv(ag_ref, cg_ref, r0)).reshape(row_chunk, tn)
    au1_ref[CONV_PAD:CONV_PAD + half, :] = jnp.dot(xn_ref[half:tm, :], wu_ref[...],
                                                   preferred_element_type=F32)
    up_epilogue(au0_ref, 0)
    up_epilogue(au1_ref, half)

    @pl.when((i % tiles_per_seq) == tiles_per_seq - 1)
    def _():
        csg_ref[0] = ag_ref[CONV_PAD + tm - 2:CONV_PAD + tm, :]
        csu_ref[0] = au1_ref[CONV_PAD + half - 2:CONV_PAD + half, :]


def _ffn_up(xn, w_up, conv_taps, xn_s, conv_state2d, *, batch, seq, tm, tn):
    m, d = xn.shape
    nb = xn_s.shape[0]
    n_col = D_FF // tn
    tiles_per_seq = seq // tm
    sample_cols = lambda k: pl.BlockSpec((nb, tn), lambda j, i: (0, k * n_col + j))
    sample_out = pl.BlockSpec((nb, tn), lambda j, i: (0, j))
    return pl.pallas_call(
        functools.partial(_ffn_up_kernel, tiles_per_seq=tiles_per_seq, row_chunk=64),
        grid=(n_col, m // tm),
        in_specs=[
            pl.BlockSpec((tm, d), lambda j, i: (i, 0)),
            pl.BlockSpec((d, tn), lambda j, i: (0, j)),
            pl.BlockSpec((d, tn), lambda j, i: (0, n_col + j)),
            pl.BlockSpec((4, 8, tn), lambda j, i: (0, 0, j)),
            pl.BlockSpec((4, 8, tn), lambda j, i: (0, 0, n_col + j)),
            pl.BlockSpec((nb, d), lambda j, i: (0, 0)),
            sample_cols(0), sample_cols(1), sample_cols(2), sample_cols(3),
        ],
        out_specs=[
            pl.BlockSpec((tm, tn), lambda j, i: (i, j)),
            pl.BlockSpec((1, 2, tn), lambda j, i: (i // tiles_per_seq, 0, j)),
            pl.BlockSpec((1, 2, tn), lambda j, i: (i // tiles_per_seq, 0, j)),
            sample_out, sample_out, sample_out,
        ],
        out_shape=[
            jax.ShapeDtypeStruct((m, D_FF), BF16),
            jax.ShapeDtypeStruct((batch, 2, D_FF), F32),
            jax.ShapeDtypeStruct((batch, 2, D_FF), F32),
            jax.ShapeDtypeStruct((nb, D_FF), F32),
            jax.ShapeDtypeStruct((nb, D_FF), F32),
            jax.ShapeDtypeStruct((nb, D_FF), BF16),
        ],
        scratch_shapes=[pltpu.VMEM((tm + CONV_PAD, tn), F32),
                        pltpu.VMEM((tm // 2 + CONV_PAD, tn), F32),
                        pltpu.VMEM((tm // 2 + CONV_PAD, tn), F32),
                        pltpu.VMEM((tm, tn), F32),
                        pltpu.VMEM((d, tn), BF16),
                        pltpu.VMEM((d, tn), BF16)],
        compiler_params=_params(("arbitrary", "arbitrary")),
        name="ffn_up",
    )(xn, w_up, w_up, conv_taps, conv_taps, xn_s, conv_state2d, conv_state2d, conv_state2d,
      conv_state2d)


def _matmul_residual_kernel(a_ref, w_ref, r_ref, o_ref):
    o_ref[...] = r_ref[...] + jnp.dot(a_ref[...], w_ref[...], preferred_element_type=F32)


def _matmul_residual(a, w, res, *, tm, tn, name):
    m, k = a.shape
    n = w.shape[1]
    return pl.pallas_call(
        _matmul_residual_kernel,
        grid=(m // tm, n // tn),
        in_specs=[
            pl.BlockSpec((tm, k), lambda i, j: (i, 0)),
            pl.BlockSpec((k, tn), lambda i, j: (0, j)),
            pl.BlockSpec((tm, tn), lambda i, j: (i, j)),
        ],
        out_specs=pl.BlockSpec((tm, tn), lambda i, j: (i, j)),
        out_shape=jax.ShapeDtypeStruct((m, n), F32),
        compiler_params=_params(("arbitrary", "arbitrary")),
        name=name,
    )(a, w, res)


def _sample_ffn_down_kernel(act_ref, w_ref, h_ref, o_ref):
    @pl.when(pl.program_id(0) == 0)
    def _():
        o_ref[...] = h_ref[...]

    o_ref[...] += jnp.dot(act_ref[...], w_ref[...], preferred_element_type=F32)


def _sample_ffn_down(act, w_down, h1, *, tk):
    m, d = h1.shape
    return pl.pallas_call(
        _sample_ffn_down_kernel,
        grid=(D_FF // tk,),
        in_specs=[
            pl.BlockSpec((m, tk), lambda k: (0, k)),
            pl.BlockSpec((tk, d), lambda k: (k, 0)),
            pl.BlockSpec((m, d), lambda k: (0, 0)),
        ],
        out_specs=pl.BlockSpec((m, d), lambda k: (0, 0)),
        out_shape=jax.ShapeDtypeStruct((m, d), F32),
        compiler_params=_params(("arbitrary",)),
        name="sample_ffn_down",
    )(act, w_down, h1)


def _ple_final_kernel(h_ref, p_ref, gple_ref, wg_ref, wp_ref, gfin_ref, y_ref, xn_ref, *,
                      row_block, row_chunk):
    tm = h_ref.shape[0]
    for r_lo in range(0, tm, row_block):
        rows = slice(r_lo, r_lo + row_block)
        for r0 in range(r_lo, r_lo + row_block, row_chunk):
            sub = slice(r0, r0 + row_chunk)
            xn_ref[sub, :] = _rms_rows(h_ref[sub, :], gple_ref[...]).astype(BF16)
        gate = jax.nn.sigmoid(jnp.dot(xn_ref[rows, :], wg_ref[...], preferred_element_type=F32))
        proj = jnp.dot(p_ref[rows, :].astype(BF16), wp_ref[...], preferred_element_type=F32)
        y_ref[rows, :] = h_ref[rows, :] + proj * gate
        for r0 in range(r_lo, r_lo + row_block, row_chunk):
            sub = slice(r0, r0 + row_chunk)
            y_ref[sub, :] = _rms_rows(y_ref[sub, :], gfin_ref[...])


def _ple_final(h, p, g_ple, w_gate, w_proj, g_final, *, tm, name):
    m, d = h.shape
    pd = p.shape[1]
    return pl.pallas_call(
        functools.partial(_ple_final_kernel, row_block=min(tm, 256), row_chunk=min(tm, 32)),
        grid=(m // tm,),
        in_specs=[
            pl.BlockSpec((tm, d), lambda i: (i, 0)),
            pl.BlockSpec((tm, pd), lambda i: (i, 0)),
            pl.BlockSpec((1, d), lambda i: (0, 0)),
            pl.BlockSpec((d, d), lambda i: (0, 0)),
            pl.BlockSpec((pd, d), lambda i: (0, 0)),
            pl.BlockSpec((1, d), lambda i: (0, 0)),
        ],
        out_specs=pl.BlockSpec((tm, d), lambda i: (i, 0)),
        out_shape=jax.ShapeDtypeStruct((m, d), F32),
        scratch_shapes=[pltpu.VMEM((tm, d), BF16)],
        compiler_params=_params(("arbitrary",)),
        name=name,
    )(h, p, g_ple, w_gate, w_proj, g_final)


def _rope_tables(pos):
    half = HEAD_DIM // 2
    inv = ROPE_THETA ** (-jnp.arange(0, HEAD_DIM, 2, dtype=F32) / HEAD_DIM)
    ang = pos.astype(F32)[:, None] * inv[None, :]
    cos, sin = jnp.cos(ang), jnp.sin(ang)
    return jnp.concatenate([cos, cos], axis=-1), jnp.concatenate([-sin, sin], axis=-1)


def _decay_tables(c):
    log_g = jnp.log1p(-jnp.exp2(-5.0 - jnp.arange(RET_HEADS, dtype=F32)))
    idx = jnp.arange(c, dtype=F32)
    diff = idx[:, None] - idx[None, :]
    dmask = jnp.where(diff >= 0, jnp.exp(log_g[:, None, None] * jnp.maximum(diff, 0.0)), 0.0)
    q_dec = jnp.exp(log_g[:, None] * (idx + 1.0))[..., None]
    k_dec = jnp.exp(log_g[:, None] * (c - 1.0 - idx))[..., None]
    c_dec = jnp.exp(log_g * c)[:, None, None]
    return dmask, q_dec, k_dec, c_dec


def kernel(x_prompt, x_sample, p_prompt, p_sample, state_ret, state_conv, g_attn, w_in, gm_ln_g,
           gm_ln_b, gm_ws, gm_bs, w_o, g_ffn, w_up, conv_w, conv_b, w_down, g_ple, w_ple_gate,
           w_ple_proj, g_final):
    batch, seq, d = x_prompt.shape
    nb = x_sample.shape[0]
    assert x_sample.shape[1] == 1 and g_attn.shape[0] == 1

    w_in_b = w_in[0].astype(BF16)
    w_o_b = w_o[0].astype(BF16)
    w_down_b = w_down[0].astype(BF16)
    w_gate_b = w_ple_gate[0].astype(BF16)
    w_proj_b = w_ple_proj[0].astype(BF16)

    g_attn2, g_ffn2, g_ple2 = g_attn[0][None], g_ffn[0][None], g_ple[0][None]
    g_fin2 = g_final[None]
    lng, lnb = gm_ln_g[0][None], gm_ln_b[0][None]
    ws, bs = gm_ws[0], gm_bs[0]
    cw, cb = conv_w[0], conv_b[0][None]
    n_groups = GM_WIDTH // HEAD_DIM

    cos_p, sin_p = _rope_tables(jnp.arange(seq, dtype=jnp.int32))
    dmask, q_dec, k_dec, c_dec = _decay_tables(CHUNK)
    qdec_p = jnp.broadcast_to(q_dec, (RET_HEADS, CHUNK, HEAD_DIM))
    kdec_p = jnp.broadcast_to(k_dec, (RET_HEADS, CHUNK, HEAD_DIM))
    cdec_p = jnp.broadcast_to(c_dec, (RET_HEADS, 1, HEAD_DIM))
    bsf = jnp.broadcast_to(bs.T[:, :, None], (CHUNK, n_groups, HEAD_DIM)).reshape(CHUNK, GM_WIDTH)
    cos_s, sin_s = _rope_tables(PAST_LEN + jnp.arange(1, dtype=jnp.int32))
    dmask1, q_dec1, k_dec1, c_dec1 = _decay_tables(1)
    qdec_s = jnp.broadcast_to(q_dec1, (RET_HEADS, 1, HEAD_DIM))
    kdec_s = jnp.broadcast_to(k_dec1, (RET_HEADS, 1, HEAD_DIM))
    cdec_s = jnp.broadcast_to(c_dec1, (RET_HEADS, 1, HEAD_DIM))
    w00 = jnp.broadcast_to(ws[:, 0, 0][:, None], (n_groups, HEAD_DIM)).reshape(1, GM_WIDTH)
    bs0 = jnp.broadcast_to(bs[:, 0][:, None], (n_groups, HEAD_DIM)).reshape(1, GM_WIDTH)
    conv_taps = jnp.broadcast_to(jnp.concatenate([cw, cb], axis=0)[:, None, :], (4, 8, 2 * D_FF))

    xp = x_prompt.reshape(batch * seq, d)
    xs = x_sample.reshape(nb, d)

    z_p = _norm_matmul(xp, g_attn2, w_in_b, tm=1024, tn=1024, name="prompt_in_proj")
    mix_p, ret_p = _prompt_mixer(z_p, batch, seq, cos_p, sin_p, dmask, qdec_p, kdec_p, cdec_p,
                                 lng, lnb, ws, bsf)
    h1_p, xn2_p = _attn_out(mix_p, w_o_b, xp, g_ffn2, tm=512, name="prompt_attn_out")

    z_s = _norm_matmul(xs, g_attn2, w_in_b, tm=nb, tn=1024, name="sample_in_proj")
    mix_s, ret_s, vn_s = _sample_mixer(z_s, state_ret[0], cos_s, sin_s, dmask1, qdec_s, kdec_s,
                                       cdec_s, lng, lnb, w00, bs0, tb=8)
    h1_s, xn2_s = _attn_out(mix_s, w_o_b, xs, g_ffn2, tm=nb, name="sample_attn_out")

    conv_state2d = state_conv[0].reshape(nb, 2 * 2 * D_FF)
    act_p, csg_p, csu_p, asg_s, asu_s, act_s = _ffn_up(
        xn2_p, w_up[0], conv_taps, xn2_s, conv_state2d, batch=batch, seq=seq, tm=1024, tn=512)
    h2_p = _matmul_residual(act_p, w_down_b, h1_p, tm=1024, tn=512, name="prompt_ffn_down")
    h2_s = _sample_ffn_down(act_s, w_down_b, h1_s, tk=512)

    y_p = _ple_final(h2_p, p_prompt[0].reshape(batch * seq, PLE_DIM), g_ple2, w_gate_b, w_proj_b,
                     g_fin2, tm=256, name="prompt_ple_final")
    y_s = _ple_final(h2_s, p_sample[0].reshape(nb, PLE_DIM), g_ple2, w_gate_b, w_proj_b, g_fin2,
                     tm=nb, name="sample_ple_final")

    conv_p = jnp.concatenate([csg_p, csu_p], axis=-1)[None]
    a_s = jnp.concatenate([asg_s, asu_s], axis=-1)
    conv_s = jnp.stack([state_conv[0][:, 1, :], a_s], axis=1)[None]
    return (y_p.reshape(batch, seq, d), y_s.reshape(nb, 1, d), ret_p[None], conv_p,
            ret_s[None], conv_s, vn_s.reshape(1, nb, 1, GM_WIDTH))
```

```python
import functools
import math

import jax
import jax.numpy as jnp
from jax import lax
from jax.experimental import pallas as pl
from jax.experimental.pallas import tpu as pltpu

F32 = jnp.float32
BF16 = jnp.bfloat16

D_MODEL = 2048
RET_HEADS = 8
HEAD_DIM = 128
CHUNK = 128
RET_QK = RET_HEADS * HEAD_DIM
GM_WIDTH = 1024
IN_COLS = 6144
D_FF = 5632
PLE_DIM = 256
ROPE_THETA = 10000.0
PAST_LEN = 16384
EPS = 1e-6
K_SCALE = HEAD_DIM ** -0.5

OFF_Q, OFF_K, OFF_V, OFF_G, OFF_U, OFF_VG = 0, 1024, 2048, 3072, 4096, 5120

VMEM_LIMIT = 56 * 1024 * 1024
LANES = 128
IN_MIXER_VMEM_LIMIT = 62 * 1024 * 1024
WEIGHT_CAST_ROWS = 256


def _params(semantics):
    return pltpu.CompilerParams(dimension_semantics=semantics, vmem_limit_bytes=VMEM_LIMIT)


def _rms_rows(x, g):
    ms = jnp.mean(x * x, axis=-1, keepdims=True)
    return x * lax.rsqrt(ms + EPS) * g


def _gelu(x):
    return jax.nn.gelu(x)


def _silu(x):
    return x * jax.nn.sigmoid(x)


def _norm_matmul_kernel(x_ref, g_ref, w_ref, o_ref, xn_ref, *, row_chunk):
    @pl.when(pl.program_id(1) == 0)
    def _():
        def body(r, carry):
            rows = pl.ds(pl.multiple_of(r * row_chunk, row_chunk), row_chunk)
            xn_ref[rows, :] = _rms_rows(x_ref[rows, :], g_ref[...]).astype(BF16)
            return carry
        lax.fori_loop(0, x_ref.shape[0] // row_chunk, body, 0)

    o_ref[...] = jnp.dot(xn_ref[...], w_ref[...], preferred_element_type=F32)


def _norm_matmul(x, g, w, *, tm, tn, name):
    m, k = x.shape
    n = w.shape[1]
    row_chunk = min(tm, 32)
    return pl.pallas_call(
        functools.partial(_norm_matmul_kernel, row_chunk=row_chunk),
        grid=(m // tm, n // tn),
        in_specs=[
            pl.BlockSpec((tm, k), lambda i, j: (i, 0)),
            pl.BlockSpec((1, k), lambda i, j: (0, 0)),
            pl.BlockSpec((k, tn), lambda i, j: (0, j)),
        ],
        out_specs=pl.BlockSpec((tm, tn), lambda i, j: (i, j)),
        out_shape=jax.ShapeDtypeStruct((m, n), F32),
        scratch_shapes=[pltpu.VMEM((tm, k), BF16)],
        compiler_params=_params(("arbitrary", "arbitrary")),
        name=name,
    )(x, g, w)


def _rope(x, cosf, sinf):
    return x * cosf + pltpu.roll(x, HEAD_DIM // 2, 1) * sinf


def _layernorm_rows(x, g, b):
    mu = jnp.mean(x, axis=-1, keepdims=True)
    xc = x - mu
    var = jnp.mean(xc * xc, axis=-1, keepdims=True)
    return xc * lax.rsqrt(var + EPS) * g + b


def _mixer_chunk(z_ref, r0, cosf, sinf, dmask_ref, qdec_ref, kdec_ref, cdec_ref, lng_ref, lnb_ref,
                 wm_ref, bsf_ref, mix_ref, out_r0, s_ref):
    zr = slice(r0, r0 + CHUNK)
    orows = slice(out_r0, out_r0 + CHUNK)
    for h in range(RET_HEADS):
        q = _rope(z_ref[zr, OFF_Q + h * HEAD_DIM:OFF_Q + (h + 1) * HEAD_DIM], cosf, sinf)
        k = _rope(z_ref[zr, OFF_K + h * HEAD_DIM:OFF_K + (h + 1) * HEAD_DIM], cosf, sinf) * K_SCALE
        v = z_ref[zr, OFF_V + h * HEAD_DIM:OFF_V + (h + 1) * HEAD_DIM].astype(BF16)
        g = z_ref[zr, OFF_G + h * HEAD_DIM:OFF_G + (h + 1) * HEAD_DIM]
        s_old = s_ref[0, h]
        sc = lax.dot_general(q.astype(BF16), k.astype(BF16), (((1,), (1,)), ((), ())),
                             preferred_element_type=F32) * dmask_ref[h]
        o = (jnp.dot(sc.astype(BF16), v, preferred_element_type=F32)
             + jnp.dot((q * qdec_ref[h]).astype(BF16), s_old.astype(BF16),
                       preferred_element_type=F32))
        s_ref[0, h] = s_old * cdec_ref[h] + lax.dot_general(
            (k * kdec_ref[h]).astype(BF16), v, (((0,), (0,)), ((), ())),
            preferred_element_type=F32)
        o = o * lax.rsqrt(jnp.mean(o * o, axis=-1, keepdims=True) + EPS)
        mix_ref[orows, h * HEAD_DIM:(h + 1) * HEAD_DIM] = (o * _silu(g)).astype(mix_ref.dtype)

    vn = _layernorm_rows(_gelu(z_ref[zr, OFF_VG:OFF_VG + GM_WIDTH]), lng_ref[...], lnb_ref[...])
    for grp in range(GM_WIDTH // HEAD_DIM):
        cols = slice(grp * HEAD_DIM, (grp + 1) * HEAD_DIM)
        sp = jnp.dot(wm_ref[grp], vn[:, cols].astype(BF16), preferred_element_type=F32) + bsf_ref[:, cols]
        u = _gelu(z_ref[zr, OFF_U + grp * HEAD_DIM:OFF_U + (grp + 1) * HEAD_DIM])
        mix_ref[orows, RET_QK + grp * HEAD_DIM:RET_QK + (grp + 1) * HEAD_DIM] = (
            (u * sp).astype(mix_ref.dtype))


def _in_mixer_kernel(x_ref, g_ref, w_ref, cos_ref, sin_ref, dmask_ref, qdec_ref, kdec_ref, cdec_ref,
                     lng_ref, lnb_ref, ws_ref, bsf_ref, mix_ref, s_ref,
                     xn_a, xn_b, z_a, z_b, wm_ref, *, steps_per_seq, row_chunk):
    step = pl.program_id(0)

    @pl.when(step % steps_per_seq == 0)
    def _():
        s_ref[...] = jnp.zeros_like(s_ref)

    @pl.when(step == 0)
    def _():
        row = lax.broadcasted_iota(jnp.int32, (CHUNK, CHUNK), 0)
        col = lax.broadcasted_iota(jnp.int32, (CHUNK, CHUNK), 1)
        for grp in range(GM_WIDTH // HEAD_DIM):
            wm_ref[grp] = jnp.where(row >= col, ws_ref[grp], 0.0).astype(BF16)

    half = x_ref.shape[0] // 2
    for xn_ref, z_ref, base in ((xn_a, z_a, 0), (xn_b, z_b, half)):
        for r0 in range(0, half, row_chunk):
            xn_ref[r0:r0 + row_chunk, :] = _rms_rows(
                x_ref[base + r0:base + r0 + row_chunk, :], g_ref[...]).astype(BF16)
        z_ref[...] = jnp.dot(xn_ref[...], w_ref[...], preferred_element_type=F32)
    for z_ref, base in ((z_a, 0), (z_b, half)):
        for c0 in range(0, half, CHUNK):
            pos = slice(base + c0, base + c0 + CHUNK)
            _mixer_chunk(z_ref, c0, cos_ref[pos, :], sin_ref[pos, :], dmask_ref, qdec_ref, kdec_ref,
                         cdec_ref, lng_ref, lnb_ref, wm_ref, bsf_ref, mix_ref, base + c0, s_ref)


def _in_mixer(x, g, w, batch, seq, cosf, sinf, dmask, qdec, kdec, cdec, lng, lnb, ws, bsf, *, tm):
    m, d = x.shape
    n = w.shape[1]
    steps_per_seq = seq // tm
    n_groups = GM_WIDTH // HEAD_DIM
    once = pl.Buffered(1)
    const3 = lambda s: (0, 0, 0)
    const2 = lambda s: (0, 0)
    return pl.pallas_call(
        functools.partial(_in_mixer_kernel, steps_per_seq=steps_per_seq, row_chunk=32),
        grid=(m // tm,),
        in_specs=[
            pl.BlockSpec((tm, d), lambda s: (s, 0)),
            pl.BlockSpec((1, d), const2, pipeline_mode=once),
            pl.BlockSpec((d, n), const2, pipeline_mode=once),
            pl.BlockSpec((tm, HEAD_DIM), lambda s: (s % steps_per_seq, 0)),
            pl.BlockSpec((tm, HEAD_DIM), lambda s: (s % steps_per_seq, 0)),
            pl.BlockSpec((RET_HEADS, CHUNK, CHUNK), const3, pipeline_mode=once),
            pl.BlockSpec((RET_HEADS, CHUNK, HEAD_DIM), const3, pipeline_mode=once),
            pl.BlockSpec((RET_HEADS, CHUNK, HEAD_DIM), const3, pipeline_mode=once),
            pl.BlockSpec((RET_HEADS, 1, HEAD_DIM), const3, pipeline_mode=once),
            pl.BlockSpec((1, GM_WIDTH), const2, pipeline_mode=once),
            pl.BlockSpec((1, GM_WIDTH), const2, pipeline_mode=once),
            pl.BlockSpec((n_groups, CHUNK, CHUNK), const3, pipeline_mode=once),
            pl.BlockSpec((CHUNK, GM_WIDTH), const2, pipeline_mode=once),
        ],
        out_specs=[
            pl.BlockSpec((tm, D_MODEL), lambda s: (s, 0)),
            pl.BlockSpec((1, RET_HEADS, HEAD_DIM, HEAD_DIM), lambda s: (s // steps_per_seq, 0, 0, 0)),
        ],
        out_shape=[
            jax.ShapeDtypeStruct((m, D_MODEL), BF16),
            jax.ShapeDtypeStruct((batch, RET_HEADS, HEAD_DIM, HEAD_DIM), F32),
        ],
        scratch_shapes=[
            pltpu.VMEM((tm // 2, d), BF16), pltpu.VMEM((tm // 2, d), BF16),
            pltpu.VMEM((tm // 2, n), F32), pltpu.VMEM((tm // 2, n), F32),
            pltpu.VMEM((n_groups, CHUNK, CHUNK), BF16),
        ],
        compiler_params=pltpu.CompilerParams(dimension_semantics=("arbitrary",),
                                             vmem_limit_bytes=IN_MIXER_VMEM_LIMIT),
        name="prompt_in_mixer",
    )(x, g, w, cosf, sinf, dmask, qdec, kdec, cdec, lng, lnb, ws, bsf)


def _sample_mixer_kernel(z_ref, s_ref, cos_ref, sin_ref, dmask_ref, qdec_ref, kdec_ref, cdec_ref,
                         lng_ref, lnb_ref, w00_ref, bs0_ref, mix_ref, so_ref, vn_ref, *, tb):
    cosf = cos_ref[...]
    sinf = sin_ref[...]
    row = lax.broadcasted_iota(jnp.int32, (HEAD_DIM, HEAD_DIM), 0)
    col = lax.broadcasted_iota(jnp.int32, (HEAD_DIM, HEAD_DIM), 1)
    eye = row == col
    ones = jnp.ones((HEAD_DIM, HEAD_DIM), BF16)

    def lane_broadcast_columns(rows):
        diag = jnp.concatenate(
            [jnp.where(eye, jnp.broadcast_to(rows[i:i + 1, :], (HEAD_DIM, HEAD_DIM)), 0.0)
             for i in range(rows.shape[0])], axis=0)
        hi = diag.astype(BF16)
        lo = (diag - hi.astype(F32)).astype(BF16)
        return (jnp.dot(hi, ones, preferred_element_type=F32)
                + jnp.dot(lo, ones, preferred_element_type=F32))

    for h in range(RET_HEADS):
        cols = slice(h * HEAD_DIM, (h + 1) * HEAD_DIM)
        q = _rope(z_ref[:, OFF_Q + h * HEAD_DIM:OFF_Q + (h + 1) * HEAD_DIM], cosf, sinf)
        k = _rope(z_ref[:, OFF_K + h * HEAD_DIM:OFF_K + (h + 1) * HEAD_DIM], cosf, sinf) * K_SCALE
        v = z_ref[:, OFF_V + h * HEAD_DIM:OFF_V + (h + 1) * HEAD_DIM]
        g = z_ref[:, OFF_G + h * HEAD_DIM:OFF_G + (h + 1) * HEAD_DIM]
        qdec = qdec_ref[h]
        kdec = kdec_ref[h]
        cdec = cdec_ref[h]
        sc = jnp.sum(q * k, axis=-1, keepdims=True) * dmask_ref[h]
        qk_cols = lane_broadcast_columns(jnp.concatenate([q * qdec, k * kdec], axis=0))
        o_rows = []
        for b in range(tb):
            s_old = s_ref[b, h]
            q_col = qk_cols[b * HEAD_DIM:(b + 1) * HEAD_DIM, :]
            k_col = qk_cols[(tb + b) * HEAD_DIM:(tb + b + 1) * HEAD_DIM, :]
            v_row = v[b:b + 1, :]
            o_rows.append(jnp.sum(q_col * s_old, axis=0, keepdims=True))
            so_ref[b, h] = s_old * cdec + k_col * v_row
        o = sc * v + jnp.concatenate(o_rows, axis=0)
        o = o * lax.rsqrt(jnp.mean(o * o, axis=-1, keepdims=True) + EPS)
        mix_ref[:, cols] = o * _silu(g)

    vn = _layernorm_rows(_gelu(z_ref[:, OFF_VG:OFF_VG + GM_WIDTH]), lng_ref[...], lnb_ref[...])
    vn_ref[...] = vn
    u = _gelu(z_ref[:, OFF_U:OFF_U + GM_WIDTH])
    mix_ref[:, RET_QK:RET_QK + GM_WIDTH] = u * (w00_ref[...] * vn + bs0_ref[...])


def _sample_mixer(z, state, cosf, sinf, dmask, qdec, kdec, cdec, lng, lnb, w00, bs0, *, tb):
    nb = z.shape[0]
    const3 = lambda i: (0, 0, 0)
    const2 = lambda i: (0, 0)
    state_spec = pl.BlockSpec((tb, RET_HEADS, HEAD_DIM, HEAD_DIM), lambda i: (i, 0, 0, 0))
    return pl.pallas_call(
        functools.partial(_sample_mixer_kernel, tb=tb),
        grid=(nb // tb,),
        in_specs=[
            pl.BlockSpec((tb, IN_COLS), lambda i: (i, 0)),
            state_spec,
            pl.BlockSpec((1, HEAD_DIM), const2),
            pl.BlockSpec((1, HEAD_DIM), const2),
            pl.BlockSpec((RET_HEADS, 1, 1), const3),
            pl.BlockSpec((RET_HEADS, 1, HEAD_DIM), const3),
            pl.BlockSpec((RET_HEADS, 1, HEAD_DIM), const3),
            pl.BlockSpec((RET_HEADS, 1, HEAD_DIM), const3),
            pl.BlockSpec((1, GM_WIDTH), const2),
            pl.BlockSpec((1, GM_WIDTH), const2),
            pl.BlockSpec((1, GM_WIDTH), const2),
            pl.BlockSpec((1, GM_WIDTH), const2),
        ],
        out_specs=[
            pl.BlockSpec((tb, D_MODEL), lambda i: (i, 0)),
            state_spec,
            pl.BlockSpec((tb, GM_WIDTH), lambda i: (i, 0)),
        ],
        out_shape=[
            jax.ShapeDtypeStruct((nb, D_MODEL), F32),
            jax.ShapeDtypeStruct(state.shape, F32),
            jax.ShapeDtypeStruct((nb, GM_WIDTH), F32),
        ],
        compiler_params=_params(("arbitrary",)),
        name="sample_mixer",
    )(z, state, cosf, sinf, dmask, qdec, kdec, cdec, lng, lnb, w00, bs0)


def _attn_out_kernel(mix_ref, w_ref, h_ref, g_ref, h1_ref, xn_ref, *, row_block, row_chunk):
    tm = h_ref.shape[0]
    for r_lo in range(0, tm, row_block):
        rows = slice(r_lo, r_lo + row_block)
        h1_ref[rows, :] = h_ref[rows, :] + jnp.dot(mix_ref[rows, :].astype(BF16), w_ref[...],
                                                   preferred_element_type=F32)
        for r0 in range(r_lo, r_lo + row_block, row_chunk):
            sub = slice(r0, r0 + row_chunk)
            xn_ref[sub, :] = _rms_rows(h1_ref[sub, :], g_ref[...]).astype(BF16)


def _attn_out(mix, w, h, g, *, tm, name):
    m, d = h.shape
    return pl.pallas_call(
        functools.partial(_attn_out_kernel, row_block=min(tm, 256), row_chunk=min(tm, 32)),
        grid=(m // tm,),
        in_specs=[
            pl.BlockSpec((tm, d), lambda i: (i, 0)),
            pl.BlockSpec((d, d), lambda i: (0, 0)),
            pl.BlockSpec((tm, d), lambda i: (i, 0)),
            pl.BlockSpec((1, d), lambda i: (0, 0)),
        ],
        out_specs=[
            pl.BlockSpec((tm, d), lambda i: (i, 0)),
            pl.BlockSpec((tm, d), lambda i: (i, 0)),
        ],
        out_shape=[
            jax.ShapeDtypeStruct((m, d), F32),
            jax.ShapeDtypeStruct((m, d), BF16),
        ],
        compiler_params=_params(("arbitrary",)),
        name=name,
    )(mix, w, h, g)


CONV_PAD = 8


def _ffn_up_kernel(xn_ref, wg32_ref, wu32_ref, cg_ref, cu_ref, xs_ref, s0g_ref, s0u_ref, s1g_ref,
                   s1u_ref, act_ref, csg_ref, csu_ref, asg_ref, asu_ref, acts_ref,
                   ag_ref, au0_ref, au1_ref, sg_ref, wg_ref, wu_ref, *, tiles_per_seq, row_chunk):
    i = pl.program_id(1)
    tm, tn = act_ref.shape
    half = tm // 2
    first = (i % tiles_per_seq) == 0

    @pl.when(first)
    def _():
        ag_ref[0:CONV_PAD, :] = jnp.zeros((CONV_PAD, tn), F32)
        au0_ref[0:CONV_PAD, :] = jnp.zeros((CONV_PAD, tn), F32)

    @pl.when(jnp.logical_not(first))
    def _():
        ag_ref[0:CONV_PAD, :] = ag_ref[tm:tm + CONV_PAD, :]
        au0_ref[0:CONV_PAD, :] = au1_ref[half:half + CONV_PAD, :]

    @pl.when(i == 0)
    def _():
        k_rows = wg_ref.shape[0]
        for k0 in range(0, k_rows, WEIGHT_CAST_ROWS):
            rows = slice(k0, k0 + WEIGHT_CAST_ROWS)
            wg_ref[rows, :] = wg32_ref[rows, :].astype(BF16)
            wu_ref[rows, :] = wu32_ref[rows, :].astype(BF16)
        xs = xs_ref[...]
        nb = xs.shape[0]
        a_g = jnp.dot(xs, wg_ref[...], preferred_element_type=F32)
        a_u = jnp.dot(xs, wu_ref[...], preferred_element_type=F32)
        asg_ref[...] = a_g
        asu_ref[...] = a_u

        def conv_s(s0_ref, s1_ref, a, c_ref):
            t = lambda v: v.reshape(nb // 8, 8, tn)
            return (t(s0_ref[...]) * c_ref[0] + t(s1_ref[...]) * c_ref[1] + t(a) * c_ref[2]
                    + c_ref[3]).reshape(nb, tn)
        acts_ref[...] = (_silu(conv_s(s0g_ref, s1g_ref, a_g, cg_ref))
                         * conv_s(s0u_ref, s1u_ref, a_u, cu_ref)).astype(acts_ref.dtype)

    def conv(a_ref, c_ref, r0):
        def window(back):
            lo = r0 + CONV_PAD - back
            return a_ref[lo:lo + row_chunk, :].reshape(row_chunk // 8, 8, tn)
        return window(2) * c_ref[0] + window(1) * c_ref[1] + window(0) * c_ref[2] + c_ref[3]

    def up_epilogue(au_ref, base):
        for r0 in range(0, half, row_chunk):
            up = conv(au_ref, cu_ref, r0).reshape(row_chunk, tn)
            rows = slice(base + r0, base + r0 + row_chunk)
            act_ref[rows, :] = (sg_ref[rows, :] * up).astype(act_ref.dtype)

    ag_ref[CONV_PAD:CONV_PAD + tm, :] = jnp.dot(xn_ref[...], wg_ref[...],
                                                preferred_element_type=F32)
    au0_ref[CONV_PAD:CONV_PAD + half, :] = jnp.dot(xn_ref[0:half, :], wu_ref[...],
                                                   preferred_element_type=F32)
    au1_ref[0:CONV_PAD, :] = au0_ref[half:half + CONV_PAD, :]
    for r0 in range(0, tm, row_chunk):
        sg_ref[r0:r0 + row_chunk, :] = _silu(conv(ag_ref, cg_ref, r0)).reshape(row_chunk, tn)
    au1_ref[CONV_PAD:CONV_PAD + half, :] = jnp.dot(xn_ref[half:tm, :], wu_ref[...],
                                                   preferred_element_type=F32)
    up_epilogue(au0_ref, 0)
    up_epilogue(au1_ref, half)

    @pl.when((i % tiles_per_seq) == tiles_per_seq - 1)
    def _():
        csg_ref[0] = ag_ref[CONV_PAD + tm - 2:CONV_PAD + tm, :]
        csu_ref[0] = au1_ref[CONV_PAD + half - 2:CONV_PAD + half, :]


def _ffn_up(xn, w_up, conv_taps, xn_s, conv_state2d, *, batch, seq, tm, tn):
    m, d = xn.shape
    nb = xn_s.shape[0]
    n_col = D_FF // tn
    tiles_per_seq = seq // tm
    sample_cols = lambda k: pl.BlockSpec((nb, tn), lambda j, i: (0, k * n_col + j))
    sample_out = pl.BlockSpec((nb, tn), lambda j, i: (0, j))
    return pl.pallas_call(
        functools.partial(_ffn_up_kernel, tiles_per_seq=tiles_per_seq, row_chunk=64),
        grid=(n_col, m // tm),
        in_specs=[
            pl.BlockSpec((tm, d), lambda j, i: (i, 0)),
            pl.BlockSpec((d, tn), lambda j, i: (0, j)),
            pl.BlockSpec((d, tn), lambda j, i: (0, n_col + j)),
            pl.BlockSpec((4, 8, tn), lambda j, i: (0, 0, j)),
            pl.BlockSpec((4, 8, tn), lambda j, i: (0, 0, n_col + j)),
            pl.BlockSpec((nb, d), lambda j, i: (0, 0)),
            sample_cols(0), sample_cols(1), sample_cols(2), sample_cols(3),
        ],
        out_specs=[
            pl.BlockSpec((tm, tn), lambda j, i: (i, j)),
            pl.BlockSpec((1, 2, tn), lambda j, i: (i // tiles_per_seq, 0, j)),
            pl.BlockSpec((1, 2, tn), lambda j, i: (i // tiles_per_seq, 0, j)),
            sample_out, sample_out, sample_out,
        ],
        out_shape=[
            jax.ShapeDtypeStruct((m, D_FF), BF16),
            jax.ShapeDtypeStruct((batch, 2, D_FF), F32),
            jax.ShapeDtypeStruct((batch, 2, D_FF), F32),
            jax.ShapeDtypeStruct((nb, D_FF), F32),
            jax.ShapeDtypeStruct((nb, D_FF), F32),
            jax.ShapeDtypeStruct((nb, D_FF), BF16),
        ],
        scratch_shapes=[pltpu.VMEM((tm + CONV_PAD, tn), F32),
                        pltpu.VMEM((tm // 2 + CONV_PAD, tn), F32),
                        pltpu.VMEM((tm // 2 + CONV_PAD, tn), F32),
                        pltpu.VMEM((tm, tn), F32),
                        pltpu.VMEM((d, tn), BF16),
                        pltpu.VMEM((d, tn), BF16)],
        compiler_params=_params(("arbitrary", "arbitrary")),
        name="ffn_up",
    )(xn, w_up, w_up, conv_taps, conv_taps, xn_s, conv_state2d, conv_state2d, conv_state2d,
      conv_state2d)


def _matmul_residual_kernel(a_ref, w_ref, r_ref, o_ref):
    o_ref[...] = r_ref[...] + jnp.dot(a_ref[...], w_ref[...], preferred_element_type=F32)


def _matmul_residual(a, w, res, *, tm, tn, name):
    m, k = a.shape
    n = w.shape[1]
    return pl.pallas_call(
        _matmul_residual_kernel,
        grid=(m // tm, n // tn),
        in_specs=[
            pl.BlockSpec((tm, k), lambda i, j: (i, 0)),
            pl.BlockSpec((k, tn), lambda i, j: (0, j)),
            pl.BlockSpec((tm, tn), lambda i, j: (i, j)),
        ],
        out_specs=pl.BlockSpec((tm, tn), lambda i, j: (i, j)),
        out_shape=jax.ShapeDtypeStruct((m, n), F32),
        compiler_params=_params(("arbitrary", "arbitrary")),
        name=name,
    )(a, w, res)


def _sample_ffn_down_kernel(act_ref, w_ref, h_ref, o_ref):
    @pl.when(pl.program_id(0) == 0)
    def _():
        o_ref[...] = h_ref[...]

    o_ref[...] += jnp.dot(act_ref[...], w_ref[...], preferred_element_type=F32)


def _sample_ffn_down(act, w_down, h1, *, tk):
    m, d = h1.shape
    return pl.pallas_call(
        _sample_ffn_down_kernel,
        grid=(D_FF // tk,),
        in_specs=[
            pl.BlockSpec((m, tk), lambda k: (0, k)),
            pl.BlockSpec((tk, d), lambda k: (k, 0)),
            pl.BlockSpec((m, d), lambda k: (0, 0)),
        ],
        out_specs=pl.BlockSpec((m, d), lambda k: (0, 0)),
        out_shape=jax.ShapeDtypeStruct((m, d), F32),
        compiler_params=_params(("arbitrary",)),
        name="sample_ffn_down",
    )(act, w_down, h1)


def _ple_final_kernel(h_ref, p_ref, gple_ref, wg_ref, wp_ref, gfin_ref, y_ref, xn_ref, *,
                      row_block, row_chunk):
    tm = h_ref.shape[0]
    for r_lo in range(0, tm, row_block):
        rows = slice(r_lo, r_lo + row_block)
        for r0 in range(r_lo, r_lo + row_block, row_chunk):
            sub = slice(r0, r0 + row_chunk)
            xn_ref[sub, :] = _rms_rows(h_ref[sub, :], gple_ref[...]).astype(BF16)
        gate = jax.nn.sigmoid(jnp.dot(xn_ref[rows, :], wg_ref[...], preferred_element_type=F32))
        proj = jnp.dot(p_ref[rows, :].astype(BF16), wp_ref[...], preferred_element_type=F32)
        y_ref[rows, :] = h_ref[rows, :] + proj * gate
        for r0 in range(r_lo, r_lo + row_block, row_chunk):
            sub = slice(r0, r0 + row_chunk)
            y_ref[sub, :] = _rms_rows(y_ref[sub, :], gfin_ref[...])


def _ple_final(h, p, g_ple, w_gate, w_proj, g_final, *, tm, name):
    m, d = h.shape
    pd = p.shape[1]
    return pl.pallas_call(
        functools.partial(_ple_final_kernel, row_block=min(tm, 256), row_chunk=min(tm, 32)),
        grid=(m // tm,),
        in_specs=[
            pl.BlockSpec((tm, d), lambda i: (i, 0)),
            pl.BlockSpec((tm, pd), lambda i: (i, 0)),
            pl.BlockSpec((1, d), lambda i: (0, 0)),
            pl.BlockSpec((d, d), lambda i: (0, 0)),
            pl.BlockSpec((pd, d), lambda i: (0, 0)),
            pl.BlockSpec((1, d), lambda i: (0, 0)),
        ],
        out_specs=pl.BlockSpec((tm, d), lambda i: (i, 0)),
        out_shape=jax.ShapeDtypeStruct((m, d), F32),
        scratch_shapes=[pltpu.VMEM((tm, d), BF16)],
        compiler_params=_params(("arbitrary",)),
        name=name,
    )(h, p, g_ple, w_gate, w_proj, g_final)


def _rope_tables(pos):
    half = HEAD_DIM // 2
    inv = ROPE_THETA ** (-jnp.arange(0, HEAD_DIM, 2, dtype=F32) / HEAD_DIM)
    ang = pos.astype(F32)[:, None] * inv[None, :]
    cos, sin = jnp.cos(ang), jnp.sin(ang)
    return jnp.concatenate([cos, cos], axis=-1), jnp.concatenate([-sin, sin], axis=-1)


def _decay_tables(c):
    log_g = jnp.log1p(-jnp.exp2(-5.0 - jnp.arange(RET_HEADS, dtype=F32)))
    idx = jnp.arange(c, dtype=F32)
    diff = idx[:, None] - idx[None, :]
    dmask = jnp.where(diff >= 0, jnp.exp(log_g[:, None, None] * jnp.maximum(diff, 0.0)), 0.0)
    q_dec = jnp.exp(log_g[:, None] * (idx + 1.0))[..., None]
    k_dec = jnp.exp(log_g[:, None] * (c - 1.0 - idx))[..., None]
    c_dec = jnp.exp(log_g * c)[:, None, None]
    return dmask, q_dec, k_dec, c_dec


def kernel(x_prompt, x_sample, p_prompt, p_sample, state_ret, state_conv, g_attn, w_in, gm_ln_g,
           gm_ln_b, gm_ws, gm_bs, w_o, g_ffn, w_up, conv_w, conv_b, w_down, g_ple, w_ple_gate,
           w_ple_proj, g_final):
    batch, seq, d = x_prompt.shape
    nb = x_sample.shape[0]
    assert x_sample.shape[1] == 1 and g_attn.shape[0] == 1

    w_in_b = w_in[0].astype(BF16)
    w_o_b = w_o[0].astype(BF16)
    w_down_b = w_down[0].astype(BF16)
    w_gate_b = w_ple_gate[0].astype(BF16)
    w_proj_b = w_ple_proj[0].astype(BF16)

    g_attn2, g_ffn2, g_ple2 = g_attn[0][None], g_ffn[0][None], g_ple[0][None]
    g_fin2 = g_final[None]
    lng, lnb = gm_ln_g[0][None], gm_ln_b[0][None]
    ws, bs = gm_ws[0], gm_bs[0]
    cw, cb = conv_w[0], conv_b[0][None]
    n_groups = GM_WIDTH // HEAD_DIM

    cos_p, sin_p = _rope_tables(jnp.arange(seq, dtype=jnp.int32))
    dmask, q_dec, k_dec, c_dec = _decay_tables(CHUNK)
    qdec_p = jnp.broadcast_to(q_dec, (RET_HEADS, CHUNK, HEAD_DIM))
    kdec_p = jnp.broadcast_to(k_dec, (RET_HEADS, CHUNK, HEAD_DIM))
    cdec_p = jnp.broadcast_to(c_dec, (RET_HEADS, 1, HEAD_DIM))
    bsf = jnp.broadcast_to(bs.T[:, :, None], (CHUNK, n_groups, HEAD_DIM)).reshape(CHUNK, GM_WIDTH)
    cos_s, sin_s = _rope_tables(PAST_LEN + jnp.arange(1, dtype=jnp.int32))
    dmask1, q_dec1, k_dec1, c_dec1 = _decay_tables(1)
    qdec_s = jnp.broadcast_to(q_dec1, (RET_HEADS, 1, HEAD_DIM))
    kdec_s = jnp.broadcast_to(k_dec1, (RET_HEADS, 1, HEAD_DIM))
    cdec_s = jnp.broadcast_to(c_dec1, (RET_HEADS, 1, HEAD_DIM))
    w00 = jnp.broadcast_to(ws[:, 0, 0][:, None], (n_groups, HEAD_DIM)).reshape(1, GM_WIDTH)
    bs0 = jnp.broadcast_to(bs[:, 0][:, None], (n_groups, HEAD_DIM)).reshape(1, GM_WIDTH)
    conv_taps = jnp.broadcast_to(jnp.concatenate([cw, cb], axis=0)[:, None, :], (4, 8, 2 * D_FF))

    xp = x_prompt.reshape(batch * seq, d)
    xs = x_sample.reshape(nb, d)

    mix_p, ret_p = _in_mixer(xp, g_attn2, w_in_b, batch, seq, cos_p, sin_p, dmask, qdec_p, kdec_p,
                             cdec_p, lng, lnb, ws, bsf, tm=512)
    h1_p, xn2_p = _attn_out(mix_p, w_o_b, xp, g_ffn2, tm=512, name="prompt_attn_out")

    z_s = _norm_matmul(xs, g_attn2, w_in_b, tm=nb, tn=1024, name="sample_in_proj")
    mix_s, ret_s, vn_s = _sample_mixer(z_s, state_ret[0], cos_s, sin_s, dmask1, qdec_s, kdec_s,
                                       cdec_s, lng, lnb, w00, bs0, tb=8)
    h1_s, xn2_s = _attn_out(mix_s, w_o_b, xs, g_ffn2, tm=nb, name="sample_attn_out")

    conv_state2d = state_conv[0].reshape(nb, 2 * 2 * D_FF)
    act_p, csg_p, csu_p, asg_s, asu_s, act_s = _ffn_up(
        xn2_p, w_up[0], conv_taps, xn2_s, conv_state2d, batch=batch, seq=seq, tm=1024, tn=512)
    h2_p = _matmul_residual(act_p, w_down_b, h1_p, tm=1024, tn=512, name="prompt_ffn_down")
    h2_s = _sample_ffn_down(act_s, w_down_b, h1_s, tk=512)

    y_p = _ple_final(h2_p, p_prompt[0].reshape(batch * seq, PLE_DIM), g_ple2, w_gate_b, w_proj_b,
                     g_fin2, tm=256, name="prompt_ple_final")
    y_s = _ple_final(h2_s, p_sample[0].reshape(nb, PLE_DIM), g_ple2, w_gate_b, w_proj_b, g_fin2,
                     tm=nb, name="sample_ple_final")

    conv_p = jnp.concatenate([csg_p, csu_p], axis=-1)[None]
    a_s = jnp.concatenate([asg_s, asu_s], axis=-1)
    conv_s = jnp.stack([state_conv[0][:, 1, :], a_s], axis=1)[None]
    return (y_p.reshape(batch, seq, d), y_s.reshape(nb, 1, d), ret_p[None], conv_p,
            ret_s[None], conv_s, vn_s.reshape(1, nb, 1, GM_WIDTH))
```

```python
import functools
import math

import jax
import jax.numpy as jnp
import numpy as np
from jax import lax
from jax.experimental import pallas as pl
from jax.experimental.pallas import tpu as pltpu

F32 = jnp.float32
BF16 = jnp.bfloat16

D_MODEL = 2048
RET_HEADS = 8
HEAD_DIM = 128
CHUNK = 128
RET_QK = RET_HEADS * HEAD_DIM
GM_WIDTH = 1024
IN_COLS = 6144
D_FF = 5632
PLE_DIM = 256
ROPE_THETA = 10000.0
PAST_LEN = 16384
EPS = 1e-6
K_SCALE = HEAD_DIM ** -0.5

OFF_Q, OFF_K, OFF_V, OFF_G, OFF_U, OFF_VG = 0, 1024, 2048, 3072, 4096, 5120

VMEM_LIMIT = 56 * 1024 * 1024
LANES = 128
IN_MIXER_VMEM_LIMIT = 62 * 1024 * 1024
WEIGHT_CAST_ROWS = 256


def _params(semantics):
    return pltpu.CompilerParams(dimension_semantics=semantics, vmem_limit_bytes=VMEM_LIMIT)


def _rms_rows(x, g):
    ms = jnp.mean(x * x, axis=-1, keepdims=True)
    return x * lax.rsqrt(ms + EPS) * g


def _gelu(x):
    return jax.nn.gelu(x)


def _silu(x):
    return x * jax.nn.sigmoid(x)


def _norm_matmul_kernel(x_ref, g_ref, w_ref, o_ref, xn_ref, *, row_chunk):
    @pl.when(pl.program_id(1) == 0)
    def _():
        def body(r, carry):
            rows = pl.ds(pl.multiple_of(r * row_chunk, row_chunk), row_chunk)
            xn_ref[rows, :] = _rms_rows(x_ref[rows, :], g_ref[...]).astype(BF16)
            return carry
        lax.fori_loop(0, x_ref.shape[0] // row_chunk, body, 0)

    o_ref[...] = jnp.dot(xn_ref[...], w_ref[...], preferred_element_type=F32)


def _norm_matmul(x, g, w, *, tm, tn, name):
    m, k = x.shape
    n = w.shape[1]
    row_chunk = min(tm, 32)
    return pl.pallas_call(
        functools.partial(_norm_matmul_kernel, row_chunk=row_chunk),
        grid=(m // tm, n // tn),
        in_specs=[
            pl.BlockSpec((tm, k), lambda i, j: (i, 0)),
            pl.BlockSpec((1, k), lambda i, j: (0, 0)),
            pl.BlockSpec((k, tn), lambda i, j: (0, j)),
        ],
        out_specs=pl.BlockSpec((tm, tn), lambda i, j: (i, j)),
        out_shape=jax.ShapeDtypeStruct((m, n), F32),
        scratch_shapes=[pltpu.VMEM((tm, k), BF16)],
        compiler_params=_params(("arbitrary", "arbitrary")),
        name=name,
    )(x, g, w)


def _rope(x, cosf, sinf):
    return x * cosf + pltpu.roll(x, HEAD_DIM // 2, 1) * sinf


def _layernorm_rows(x, g, b):
    mu = jnp.mean(x, axis=-1, keepdims=True)
    xc = x - mu
    var = jnp.mean(xc * xc, axis=-1, keepdims=True)
    return xc * lax.rsqrt(var + EPS) * g + b


def _mixer_chunk(z_ref, r0, cosf, sinf, dmask_ref, qdec_ref, kdec_ref, cdec_ref, lng_ref, lnb_ref,
                 wm_ref, bsf_ref, mix_ref, out_r0, s_ref):
    zr = slice(r0, r0 + CHUNK)
    orows = slice(out_r0, out_r0 + CHUNK)
    for h in range(RET_HEADS):
        q = _rope(z_ref[zr, OFF_Q + h * HEAD_DIM:OFF_Q + (h + 1) * HEAD_DIM], cosf, sinf)
        k = _rope(z_ref[zr, OFF_K + h * HEAD_DIM:OFF_K + (h + 1) * HEAD_DIM], cosf, sinf) * K_SCALE
        v = z_ref[zr, OFF_V + h * HEAD_DIM:OFF_V + (h + 1) * HEAD_DIM].astype(BF16)
        g = z_ref[zr, OFF_G + h * HEAD_DIM:OFF_G + (h + 1) * HEAD_DIM]
        s_old = s_ref[0, h]
        sc = lax.dot_general(q.astype(BF16), k.astype(BF16), (((1,), (1,)), ((), ())),
                             preferred_element_type=F32) * dmask_ref[h]
        o = (jnp.dot(sc.astype(BF16), v, preferred_element_type=F32)
             + jnp.dot((q * qdec_ref[h]).astype(BF16), s_old.astype(BF16),
                       preferred_element_type=F32))
        s_ref[0, h] = s_old * cdec_ref[h] + lax.dot_general(
            (k * kdec_ref[h]).astype(BF16), v, (((0,), (0,)), ((), ())),
            preferred_element_type=F32)
        o = o * lax.rsqrt(jnp.mean(o * o, axis=-1, keepdims=True) + EPS)
        mix_ref[orows, h * HEAD_DIM:(h + 1) * HEAD_DIM] = (o * _silu(g)).astype(mix_ref.dtype)

    vn = _layernorm_rows(_gelu(z_ref[zr, OFF_VG:OFF_VG + GM_WIDTH]), lng_ref[...], lnb_ref[...])
    for grp in range(GM_WIDTH // HEAD_DIM):
        cols = slice(grp * HEAD_DIM, (grp + 1) * HEAD_DIM)
        sp = jnp.dot(wm_ref[grp], vn[:, cols].astype(BF16), preferred_element_type=F32) + bsf_ref[:, cols]
        u = _gelu(z_ref[zr, OFF_U + grp * HEAD_DIM:OFF_U + (grp + 1) * HEAD_DIM])
        mix_ref[orows, RET_QK + grp * HEAD_DIM:RET_QK + (grp + 1) * HEAD_DIM] = (
            (u * sp).astype(mix_ref.dtype))


def _in_mixer_kernel(x_ref, g_ref, w_ref, cos_ref, sin_ref, dmask_ref, qdec_ref, kdec_ref, cdec_ref,
                     lng_ref, lnb_ref, ws_ref, bsf_ref, mix_ref, s_ref,
                     xn_a, xn_b, z_a, z_b, wm_ref, *, steps_per_seq, row_chunk):
    step = pl.program_id(0)

    @pl.when(step % steps_per_seq == 0)
    def _():
        s_ref[...] = jnp.zeros_like(s_ref)

    @pl.when(step == 0)
    def _():
        row = lax.broadcasted_iota(jnp.int32, (CHUNK, CHUNK), 0)
        col = lax.broadcasted_iota(jnp.int32, (CHUNK, CHUNK), 1)
        for grp in range(GM_WIDTH // HEAD_DIM):
            wm_ref[grp] = jnp.where(row >= col, ws_ref[grp], 0.0).astype(BF16)

    half = x_ref.shape[0] // 2
    for xn_ref, z_ref, base in ((xn_a, z_a, 0), (xn_b, z_b, half)):
        for r0 in range(0, half, row_chunk):
            xn_ref[r0:r0 + row_chunk, :] = _rms_rows(
                x_ref[base + r0:base + r0 + row_chunk, :], g_ref[...]).astype(BF16)
        z_ref[...] = jnp.dot(xn_ref[...], w_ref[...], preferred_element_type=F32)
    for z_ref, base in ((z_a, 0), (z_b, half)):
        for c0 in range(0, half, CHUNK):
            pos = slice(base + c0, base + c0 + CHUNK)
            _mixer_chunk(z_ref, c0, cos_ref[pos, :], sin_ref[pos, :], dmask_ref, qdec_ref, kdec_ref,
                         cdec_ref, lng_ref, lnb_ref, wm_ref, bsf_ref, mix_ref, base + c0, s_ref)


def _in_mixer(x, g, w, batch, seq, cosf, sinf, dmask, qdec, kdec, cdec, lng, lnb, ws, bsf, *, tm):
    m, d = x.shape
    n = w.shape[1]
    steps_per_seq = seq // tm
    n_groups = GM_WIDTH // HEAD_DIM
    once = pl.Buffered(1)
    const3 = lambda s: (0, 0, 0)
    const2 = lambda s: (0, 0)
    return pl.pallas_call(
        functools.partial(_in_mixer_kernel, steps_per_seq=steps_per_seq, row_chunk=32),
        grid=(m // tm,),
        in_specs=[
            pl.BlockSpec((tm, d), lambda s: (s, 0)),
            pl.BlockSpec((1, d), const2, pipeline_mode=once),
            pl.BlockSpec((d, n), const2, pipeline_mode=once),
            pl.BlockSpec((tm, HEAD_DIM), lambda s: (s % steps_per_seq, 0)),
            pl.BlockSpec((tm, HEAD_DIM), lambda s: (s % steps_per_seq, 0)),
            pl.BlockSpec((RET_HEADS, CHUNK, CHUNK), const3, pipeline_mode=once),
            pl.BlockSpec((RET_HEADS, CHUNK, HEAD_DIM), const3, pipeline_mode=once),
            pl.BlockSpec((RET_HEADS, CHUNK, HEAD_DIM), const3, pipeline_mode=once),
            pl.BlockSpec((RET_HEADS, 1, HEAD_DIM), const3, pipeline_mode=once),
            pl.BlockSpec((1, GM_WIDTH), const2, pipeline_mode=once),
            pl.BlockSpec((1, GM_WIDTH), const2, pipeline_mode=once),
            pl.BlockSpec((n_groups, CHUNK, CHUNK), const3, pipeline_mode=once),
            pl.BlockSpec((CHUNK, GM_WIDTH), const2, pipeline_mode=once),
        ],
        out_specs=[
            pl.BlockSpec((tm, D_MODEL), lambda s: (s, 0)),
            pl.BlockSpec((1, RET_HEADS, HEAD_DIM, HEAD_DIM), lambda s: (s // steps_per_seq, 0, 0, 0)),
        ],
        out_shape=[
            jax.ShapeDtypeStruct((m, D_MODEL), BF16),
            jax.ShapeDtypeStruct((batch, RET_HEADS, HEAD_DIM, HEAD_DIM), F32),
        ],
        scratch_shapes=[
            pltpu.VMEM((tm // 2, d), BF16), pltpu.VMEM((tm // 2, d), BF16),
            pltpu.VMEM((tm // 2, n), F32), pltpu.VMEM((tm // 2, n), F32),
            pltpu.VMEM((n_groups, CHUNK, CHUNK), BF16),
        ],
        compiler_params=pltpu.CompilerParams(dimension_semantics=("arbitrary",),
                                             vmem_limit_bytes=IN_MIXER_VMEM_LIMIT),
        name="prompt_in_mixer",
    )(x, g, w, cosf, sinf, dmask, qdec, kdec, cdec, lng, lnb, ws, bsf)


def _sample_mixer_kernel(z_ref, s_ref, cos_ref, sin_ref, dmask_ref, qdec_ref, kdec_ref, cdec_ref,
                         lng_ref, lnb_ref, w00_ref, bs0_ref, mix_ref, so_ref, vn_ref, *, tb):
    cosf = cos_ref[...]
    sinf = sin_ref[...]
    row = lax.broadcasted_iota(jnp.int32, (HEAD_DIM, HEAD_DIM), 0)
    col = lax.broadcasted_iota(jnp.int32, (HEAD_DIM, HEAD_DIM), 1)
    eye = row == col
    ones = jnp.ones((HEAD_DIM, HEAD_DIM), BF16)

    def lane_broadcast_columns(rows):
        diag = jnp.concatenate(
            [jnp.where(eye, jnp.broadcast_to(rows[i:i + 1, :], (HEAD_DIM, HEAD_DIM)), 0.0)
             for i in range(rows.shape[0])], axis=0)
        hi = diag.astype(BF16)
        lo = (diag - hi.astype(F32)).astype(BF16)
        return (jnp.dot(hi, ones, preferred_element_type=F32)
                + jnp.dot(lo, ones, preferred_element_type=F32))

    for h in range(RET_HEADS):
        cols = slice(h * HEAD_DIM, (h + 1) * HEAD_DIM)
        q = _rope(z_ref[:, OFF_Q + h * HEAD_DIM:OFF_Q + (h + 1) * HEAD_DIM], cosf, sinf)
        k = _rope(z_ref[:, OFF_K + h * HEAD_DIM:OFF_K + (h + 1) * HEAD_DIM], cosf, sinf) * K_SCALE
        v = z_ref[:, OFF_V + h * HEAD_DIM:OFF_V + (h + 1) * HEAD_DIM]
        g = z_ref[:, OFF_G + h * HEAD_DIM:OFF_G + (h + 1) * HEAD_DIM]
        qdec = qdec_ref[h]
        kdec = kdec_ref[h]
        cdec = cdec_ref[h]
        sc = jnp.sum(q * k, axis=-1, keepdims=True) * dmask_ref[h]
        qk_cols = lane_broadcast_columns(jnp.concatenate([q * qdec, k * kdec], axis=0))
        o_rows = []
        for b in range(tb):
            s_old = s_ref[b, h]
            q_col = qk_cols[b * HEAD_DIM:(b + 1) * HEAD_DIM, :]
            k_col = qk_cols[(tb + b) * HEAD_DIM:(tb + b + 1) * HEAD_DIM, :]
            v_row = v[b:b + 1, :]
            o_rows.append(jnp.sum(q_col * s_old, axis=0, keepdims=True))
            so_ref[b, h] = s_old * cdec + k_col * v_row
        o = sc * v + jnp.concatenate(o_rows, axis=0)
        o = o * lax.rsqrt(jnp.mean(o * o, axis=-1, keepdims=True) + EPS)
        mix_ref[:, cols] = o * _silu(g)

    vn = _layernorm_rows(_gelu(z_ref[:, OFF_VG:OFF_VG + GM_WIDTH]), lng_ref[...], lnb_ref[...])
    vn_ref[...] = vn
    u = _gelu(z_ref[:, OFF_U:OFF_U + GM_WIDTH])
    mix_ref[:, RET_QK:RET_QK + GM_WIDTH] = u * (w00_ref[...] * vn + bs0_ref[...])


def _sample_mixer(z, state, cosf, sinf, dmask, qdec, kdec, cdec, lng, lnb, w00, bs0, *, tb):
    nb = z.shape[0]
    const3 = lambda i: (0, 0, 0)
    const2 = lambda i: (0, 0)
    state_spec = pl.BlockSpec((tb, RET_HEADS, HEAD_DIM, HEAD_DIM), lambda i: (i, 0, 0, 0))
    return pl.pallas_call(
        functools.partial(_sample_mixer_kernel, tb=tb),
        grid=(nb // tb,),
        in_specs=[
            pl.BlockSpec((tb, IN_COLS), lambda i: (i, 0)),
            state_spec,
            pl.BlockSpec((1, HEAD_DIM), const2),
            pl.BlockSpec((1, HEAD_DIM), const2),
            pl.BlockSpec((RET_HEADS, 1, 1), const3),
            pl.BlockSpec((RET_HEADS, 1, HEAD_DIM), const3),
            pl.BlockSpec((RET_HEADS, 1, HEAD_DIM), const3),
            pl.BlockSpec((RET_HEADS, 1, HEAD_DIM), const3),
            pl.BlockSpec((1, GM_WIDTH), const2),
            pl.BlockSpec((1, GM_WIDTH), const2),
            pl.BlockSpec((1, GM_WIDTH), const2),
            pl.BlockSpec((1, GM_WIDTH), const2),
        ],
        out_specs=[
            pl.BlockSpec((tb, D_MODEL), lambda i: (i, 0)),
            state_spec,
            pl.BlockSpec((tb, GM_WIDTH), lambda i: (i, 0)),
        ],
        out_shape=[
            jax.ShapeDtypeStruct((nb, D_MODEL), F32),
            jax.ShapeDtypeStruct(state.shape, F32),
            jax.ShapeDtypeStruct((nb, GM_WIDTH), F32),
        ],
        compiler_params=_params(("arbitrary",)),
        name="sample_mixer",
    )(z, state, cosf, sinf, dmask, qdec, kdec, cdec, lng, lnb, w00, bs0)


def _attn_out_kernel(mix_ref, w_ref, h_ref, g_ref, h1_ref, xn_ref, *, row_block, row_chunk):
    tm = h_ref.shape[0]
    for r_lo in range(0, tm, row_block):
        rows = slice(r_lo, r_lo + row_block)
        h1_ref[rows, :] = h_ref[rows, :] + jnp.dot(mix_ref[rows, :].astype(BF16), w_ref[...],
                                                   preferred_element_type=F32)
        for r0 in range(r_lo, r_lo + row_block, row_chunk):
            sub = slice(r0, r0 + row_chunk)
            xn_ref[sub, :] = _rms_rows(h1_ref[sub, :], g_ref[...]).astype(BF16)


def _attn_out(mix, w, h, g, *, tm, name):
    m, d = h.shape
    return pl.pallas_call(
        functools.partial(_attn_out_kernel, row_block=min(tm, 256), row_chunk=min(tm, 32)),
        grid=(m // tm,),
        in_specs=[
            pl.BlockSpec((tm, d), lambda i: (i, 0)),
            pl.BlockSpec((d, d), lambda i: (0, 0)),
            pl.BlockSpec((tm, d), lambda i: (i, 0)),
            pl.BlockSpec((1, d), lambda i: (0, 0)),
        ],
        out_specs=[
            pl.BlockSpec((tm, d), lambda i: (i, 0)),
            pl.BlockSpec((tm, d), lambda i: (i, 0)),
        ],
        out_shape=[
            jax.ShapeDtypeStruct((m, d), F32),
            jax.ShapeDtypeStruct((m, d), BF16),
        ],
        compiler_params=_params(("arbitrary",)),
        name=name,
    )(mix, w, h, g)


CONV_PAD = 8


def _ffn_up_kernel(xn_ref, wg32_ref, wu32_ref, cg_ref, cu_ref, xs_ref, s0g_ref, s0u_ref, s1g_ref,
                   s1u_ref, act_ref, csg_ref, csu_ref, asg_ref, asu_ref, acts_ref,
                   ag_ref, sg_ref, wg_ref, wu_ref, *au_refs, tiles_per_seq, row_chunk, up_splits):
    i = pl.program_id(1)
    tm, tn = act_ref.shape
    first = (i % tiles_per_seq) == 0
    last_rows = up_splits[-1]

    @pl.when(first)
    def _():
        ag_ref[0:CONV_PAD, :] = jnp.zeros((CONV_PAD, tn), F32)
        au_refs[0][0:CONV_PAD, :] = jnp.zeros((CONV_PAD, tn), F32)

    @pl.when(jnp.logical_not(first))
    def _():
        ag_ref[0:CONV_PAD, :] = ag_ref[tm:tm + CONV_PAD, :]
        au_refs[0][0:CONV_PAD, :] = au_refs[-1][last_rows:last_rows + CONV_PAD, :]

    @pl.when(i == 0)
    def _():
        k_rows = wg_ref.shape[0]
        for k0 in range(0, k_rows, WEIGHT_CAST_ROWS):
            rows = slice(k0, k0 + WEIGHT_CAST_ROWS)
            wg_ref[rows, :] = wg32_ref[rows, :].astype(BF16)
            wu_ref[rows, :] = wu32_ref[rows, :].astype(BF16)
        xs = xs_ref[...]
        nb = xs.shape[0]
        a_g = jnp.dot(xs, wg_ref[...], preferred_element_type=F32)
        a_u = jnp.dot(xs, wu_ref[...], preferred_element_type=F32)
        asg_ref[...] = a_g
        asu_ref[...] = a_u

        def conv_s(s0_ref, s1_ref, a, c_ref):
            t = lambda v: v.reshape(nb // 8, 8, tn)
            return (t(s0_ref[...]) * c_ref[0] + t(s1_ref[...]) * c_ref[1] + t(a) * c_ref[2]
                    + c_ref[3]).reshape(nb, tn)
        acts_ref[...] = (_silu(conv_s(s0g_ref, s1g_ref, a_g, cg_ref))
                         * conv_s(s0u_ref, s1u_ref, a_u, cu_ref)).astype(acts_ref.dtype)

    def conv(a_ref, c_ref, r0):
        def window(back):
            lo = r0 + CONV_PAD - back
            return a_ref[lo:lo + row_chunk, :].reshape(row_chunk // 8, 8, tn)
        return window(2) * c_ref[0] + window(1) * c_ref[1] + window(0) * c_ref[2] + c_ref[3]

    ag_ref[CONV_PAD:CONV_PAD + tm, :] = jnp.dot(xn_ref[...], wg_ref[...],
                                                preferred_element_type=F32)
    for r0 in range(0, tm, row_chunk):
        sg_ref[r0:r0 + row_chunk, :] = _silu(conv(ag_ref, cg_ref, r0)).reshape(row_chunk, tn)
    start = 0
    for idx, rows in enumerate(up_splits):
        au_ref = au_refs[idx]
        au_ref[CONV_PAD:CONV_PAD + rows, :] = jnp.dot(xn_ref[start:start + rows, :], wu_ref[...],
                                                      preferred_element_type=F32)
        if idx + 1 < len(up_splits):
            au_refs[idx + 1][0:CONV_PAD, :] = au_ref[rows:rows + CONV_PAD, :]
        start += rows
    start = 0
    for idx, rows in enumerate(up_splits):
        for r0 in range(0, rows, row_chunk):
            up = conv(au_refs[idx], cu_ref, r0).reshape(row_chunk, tn)
            out = slice(start + r0, start + r0 + row_chunk)
            act_ref[out, :] = (sg_ref[out, :] * up).astype(act_ref.dtype)
        start += rows

    @pl.when((i % tiles_per_seq) == tiles_per_seq - 1)
    def _():
        csg_ref[0] = ag_ref[CONV_PAD + tm - 2:CONV_PAD + tm, :]
        csu_ref[0] = au_refs[-1][CONV_PAD + last_rows - 2:CONV_PAD + last_rows, :]


def _ffn_up(xn, w_up, conv_taps, xn_s, conv_state2d, *, batch, seq, tm, tn):
    m, d = xn.shape
    nb = xn_s.shape[0]
    n_col = D_FF // tn
    tiles_per_seq = seq // tm
    up_splits = (tm // 2, tm // 2)
    sample_cols = lambda k: pl.BlockSpec((nb, tn), lambda j, i: (0, k * n_col + j))
    sample_out = pl.BlockSpec((nb, tn), lambda j, i: (0, j))
    return pl.pallas_call(
        functools.partial(_ffn_up_kernel, tiles_per_seq=tiles_per_seq, row_chunk=64,
                          up_splits=up_splits),
        grid=(n_col, m // tm),
        in_specs=[
            pl.BlockSpec((tm, d), lambda j, i: (i, 0)),
            pl.BlockSpec((d, tn), lambda j, i: (0, j)),
            pl.BlockSpec((d, tn), lambda j, i: (0, n_col + j)),
            pl.BlockSpec((4, 8, tn), lambda j, i: (0, 0, j)),
            pl.BlockSpec((4, 8, tn), lambda j, i: (0, 0, n_col + j)),
            pl.BlockSpec((nb, d), lambda j, i: (0, 0)),
            sample_cols(0), sample_cols(1), sample_cols(2), sample_cols(3),
        ],
        out_specs=[
            pl.BlockSpec((tm, tn), lambda j, i: (i, j)),
            pl.BlockSpec((1, 2, tn), lambda j, i: (i // tiles_per_seq, 0, j)),
            pl.BlockSpec((1, 2, tn), lambda j, i: (i // tiles_per_seq, 0, j)),
            sample_out, sample_out, sample_out,
        ],
        out_shape=[
            jax.ShapeDtypeStruct((m, D_FF), BF16),
            jax.ShapeDtypeStruct((batch, 2, D_FF), F32),
            jax.ShapeDtypeStruct((batch, 2, D_FF), F32),
            jax.ShapeDtypeStruct((nb, D_FF), F32),
            jax.ShapeDtypeStruct((nb, D_FF), F32),
            jax.ShapeDtypeStruct((nb, D_FF), BF16),
        ],
        scratch_shapes=[pltpu.VMEM((tm + CONV_PAD, tn), F32),
                        pltpu.VMEM((tm, tn), F32),
                        pltpu.VMEM((d, tn), BF16),
                        pltpu.VMEM((d, tn), BF16)]
                       + [pltpu.VMEM((rows + CONV_PAD, tn), F32) for rows in up_splits],
        compiler_params=_params(("arbitrary", "arbitrary")),
        name="ffn_up",
    )(xn, w_up, w_up, conv_taps, conv_taps, xn_s, conv_state2d, conv_state2d, conv_state2d,
      conv_state2d)


def _ffn_down_kernel(act_ref, w32_ref, h_ref, acts_ref, hs_ref, o_ref, os_ref, w_ref):
    @pl.when(pl.program_id(1) == 0)
    def _():
        for k0 in range(0, w_ref.shape[0], WEIGHT_CAST_ROWS):
            rows = slice(k0, k0 + WEIGHT_CAST_ROWS)
            w_ref[rows, :] = w32_ref[rows, :].astype(BF16)
        os_ref[...] = hs_ref[...] + jnp.dot(acts_ref[...], w_ref[...], preferred_element_type=F32)

    o_ref[...] = h_ref[...] + jnp.dot(act_ref[...], w_ref[...], preferred_element_type=F32)


def _ffn_down(act, w_down, h1, act_s, h1_s, *, tm, tn):
    m, k = act.shape
    n = w_down.shape[1]
    nb = act_s.shape[0]
    assert k % WEIGHT_CAST_ROWS == 0
    return pl.pallas_call(
        _ffn_down_kernel,
        grid=(n // tn, m // tm),
        in_specs=[
            pl.BlockSpec((tm, k), lambda j, i: (i, 0)),
            pl.BlockSpec((k, tn), lambda j, i: (0, j)),
            pl.BlockSpec((tm, tn), lambda j, i: (i, j)),
            pl.BlockSpec((nb, k), lambda j, i: (0, 0)),
            pl.BlockSpec((nb, tn), lambda j, i: (0, j)),
        ],
        out_specs=[
            pl.BlockSpec((tm, tn), lambda j, i: (i, j)),
            pl.BlockSpec((nb, tn), lambda j, i: (0, j)),
        ],
        out_shape=[
            jax.ShapeDtypeStruct((m, n), F32),
            jax.ShapeDtypeStruct((nb, n), F32),
        ],
        scratch_shapes=[pltpu.VMEM((k, tn), BF16)],
        compiler_params=_params(("arbitrary", "arbitrary")),
        name="ffn_down",
    )(act, w_down, h1, act_s, h1_s)


def _ple_final_kernel(h_ref, p_ref, gple_ref, wg_ref, wp_ref, gfin_ref, y_ref, xn_ref, *,
                      row_block, row_chunk):
    tm = h_ref.shape[0]
    for r_lo in range(0, tm, row_block):
        rows = slice(r_lo, r_lo + row_block)
        for r0 in range(r_lo, r_lo + row_block, row_chunk):
            sub = slice(r0, r0 + row_chunk)
            xn_ref[sub, :] = _rms_rows(h_ref[sub, :], gple_ref[...]).astype(BF16)
        gate = jax.nn.sigmoid(jnp.dot(xn_ref[rows, :], wg_ref[...], preferred_element_type=F32))
        proj = jnp.dot(p_ref[rows, :].astype(BF16), wp_ref[...], preferred_element_type=F32)
        y_ref[rows, :] = h_ref[rows, :] + proj * gate
        for r0 in range(r_lo, r_lo + row_block, row_chunk):
            sub = slice(r0, r0 + row_chunk)
            y_ref[sub, :] = _rms_rows(y_ref[sub, :], gfin_ref[...])


def _ple_final(h, p, g_ple, w_gate, w_proj, g_final, *, tm, name):
    m, d = h.shape
    pd = p.shape[1]
    return pl.pallas_call(
        functools.partial(_ple_final_kernel, row_block=min(tm, 256), row_chunk=min(tm, 32)),
        grid=(m // tm,),
        in_specs=[
            pl.BlockSpec((tm, d), lambda i: (i, 0)),
            pl.BlockSpec((tm, pd), lambda i: (i, 0)),
            pl.BlockSpec((1, d), lambda i: (0, 0)),
            pl.BlockSpec((d, d), lambda i: (0, 0)),
            pl.BlockSpec((pd, d), lambda i: (0, 0)),
            pl.BlockSpec((1, d), lambda i: (0, 0)),
        ],
        out_specs=pl.BlockSpec((tm, d), lambda i: (i, 0)),
        out_shape=jax.ShapeDtypeStruct((m, d), F32),
        scratch_shapes=[pltpu.VMEM((tm, d), BF16)],
        compiler_params=_params(("arbitrary",)),
        name=name,
    )(h, p, g_ple, w_gate, w_proj, g_final)


def _rope_tables(pos):
    inv = ROPE_THETA ** (-np.arange(0, HEAD_DIM, 2, dtype=np.float64) / HEAD_DIM)
    ang = np.asarray(pos, np.float64)[:, None] * inv[None, :]
    cos, sin = np.cos(ang), np.sin(ang)
    return (jnp.asarray(np.concatenate([cos, cos], axis=-1), F32),
            jnp.asarray(np.concatenate([-sin, sin], axis=-1), F32))


def _decay_tables(c):
    log_g = np.log1p(-np.exp2(-5.0 - np.arange(RET_HEADS, dtype=np.float64)))
    idx = np.arange(c, dtype=np.float64)
    diff = idx[:, None] - idx[None, :]
    dmask = np.where(diff >= 0, np.exp(log_g[:, None, None] * np.maximum(diff, 0.0)), 0.0)
    q_dec = np.exp(log_g[:, None] * (idx + 1.0))[..., None]
    k_dec = np.exp(log_g[:, None] * (c - 1.0 - idx))[..., None]
    c_dec = np.exp(log_g * c)[:, None, None]
    return tuple(jnp.asarray(t, F32) for t in (dmask, q_dec, k_dec, c_dec))


def kernel(x_prompt, x_sample, p_prompt, p_sample, state_ret, state_conv, g_attn, w_in, gm_ln_g,
           gm_ln_b, gm_ws, gm_bs, w_o, g_ffn, w_up, conv_w, conv_b, w_down, g_ple, w_ple_gate,
           w_ple_proj, g_final):
    batch, seq, d = x_prompt.shape
    nb = x_sample.shape[0]
    assert x_sample.shape[1] == 1 and g_attn.shape[0] == 1

    w_in_b = w_in[0].astype(BF16)
    w_o_b = w_o[0].astype(BF16)
    w_gate_b = w_ple_gate[0].astype(BF16)
    w_proj_b = w_ple_proj[0].astype(BF16)

    g_attn2, g_ffn2, g_ple2 = g_attn[0][None], g_ffn[0][None], g_ple[0][None]
    g_fin2 = g_final[None]
    lng, lnb = gm_ln_g[0][None], gm_ln_b[0][None]
    ws, bs = gm_ws[0], gm_bs[0]
    cw, cb = conv_w[0], conv_b[0][None]
    n_groups = GM_WIDTH // HEAD_DIM

    cos_p, sin_p = _rope_tables(np.arange(seq))
    dmask, q_dec, k_dec, c_dec = _decay_tables(CHUNK)
    qdec_p = jnp.broadcast_to(q_dec, (RET_HEADS, CHUNK, HEAD_DIM))
    kdec_p = jnp.broadcast_to(k_dec, (RET_HEADS, CHUNK, HEAD_DIM))
    cdec_p = jnp.broadcast_to(c_dec, (RET_HEADS, 1, HEAD_DIM))
    bsf = jnp.broadcast_to(bs.T[:, :, None], (CHUNK, n_groups, HEAD_DIM)).reshape(CHUNK, GM_WIDTH)
    cos_s, sin_s = _rope_tables(PAST_LEN + np.arange(1))
    dmask1, q_dec1, k_dec1, c_dec1 = _decay_tables(1)
    qdec_s = jnp.broadcast_to(q_dec1, (RET_HEADS, 1, HEAD_DIM))
    kdec_s = jnp.broadcast_to(k_dec1, (RET_HEADS, 1, HEAD_DIM))
    cdec_s = jnp.broadcast_to(c_dec1, (RET_HEADS, 1, HEAD_DIM))
    w00 = jnp.broadcast_to(ws[:, 0, 0][:, None], (n_groups, HEAD_DIM)).reshape(1, GM_WIDTH)
    bs0 = jnp.broadcast_to(bs[:, 0][:, None], (n_groups, HEAD_DIM)).reshape(1, GM_WIDTH)
    conv_taps = jnp.broadcast_to(jnp.concatenate([cw, cb], axis=0)[:, None, :], (4, 8, 2 * D_FF))

    xp = x_prompt.reshape(batch * seq, d)
    xs = x_sample.reshape(nb, d)

    mix_p, ret_p = _in_mixer(xp, g_attn2, w_in_b, batch, seq, cos_p, sin_p, dmask, qdec_p, kdec_p,
                             cdec_p, lng, lnb, ws, bsf, tm=512)
    h1_p, xn2_p = _attn_out(mix_p, w_o_b, xp, g_ffn2, tm=512, name="prompt_attn_out")

    z_s = _norm_matmul(xs, g_attn2, w_in_b, tm=nb, tn=1024, name="sample_in_proj")
    mix_s, ret_s, vn_s = _sample_mixer(z_s, state_ret[0], cos_s, sin_s, dmask1, qdec_s, kdec_s,
                                       cdec_s, lng, lnb, w00, bs0, tb=8)
    h1_s, xn2_s = _attn_out(mix_s, w_o_b, xs, g_ffn2, tm=nb, name="sample_attn_out")

    conv_state2d = state_conv[0].reshape(nb, 2 * 2 * D_FF)
    act_p, csg_p, csu_p, asg_s, asu_s, act_s = _ffn_up(
        xn2_p, w_up[0], conv_taps, xn2_s, conv_state2d, batch=batch, seq=seq, tm=1024, tn=512)
    h2_p, h2_s = _ffn_down(act_p, w_down[0], h1_p, act_s, h1_s, tm=512, tn=512)

    y_p = _ple_final(h2_p, p_prompt[0].reshape(batch * seq, PLE_DIM), g_ple2, w_gate_b, w_proj_b,
                     g_fin2, tm=512, name="prompt_ple_final")
    y_s = _ple_final(h2_s, p_sample[0].reshape(nb, PLE_DIM), g_ple2, w_gate_b, w_proj_b, g_fin2,
                     tm=nb, name="sample_ple_final")

    conv_p = jnp.concatenate([csg_p, csu_p], axis=-1)[None]
    a_s = jnp.concatenate([asg_s, asu_s], axis=-1)
    conv_s = jnp.stack([state_conv[0][:, 1, :], a_s], axis=1)[None]
    return (y_p.reshape(batch, seq, d), y_s.reshape(nb, 1, d), ret_p[None], conv_p,
            ret_s[None], conv_s, vn_s.reshape(1, nb, 1, GM_WIDTH))
```

```python
import functools
import math

import jax
import jax.numpy as jnp
import numpy as np
from jax import lax
from jax.experimental import pallas as pl
from jax.experimental.pallas import tpu as pltpu

F32 = jnp.float32
BF16 = jnp.bfloat16

D_MODEL = 2048
RET_HEADS = 8
HEAD_DIM = 128
CHUNK = 128
RET_QK = RET_HEADS * HEAD_DIM
GM_WIDTH = 1024
IN_COLS = 6144
D_FF = 5632
PLE_DIM = 256
ROPE_THETA = 10000.0
PAST_LEN = 16384
EPS = 1e-6
K_SCALE = HEAD_DIM ** -0.5

OFF_Q, OFF_K, OFF_V, OFF_G, OFF_U, OFF_VG = 0, 1024, 2048, 3072, 4096, 5120

VMEM_LIMIT = 56 * 1024 * 1024
LANES = 128
IN_MIXER_VMEM_LIMIT = 62 * 1024 * 1024
WEIGHT_CAST_ROWS = 256


def _params(semantics):
    return pltpu.CompilerParams(dimension_semantics=semantics, vmem_limit_bytes=VMEM_LIMIT)


def _rms_rows(x, g):
    ms = jnp.mean(x * x, axis=-1, keepdims=True)
    return x * lax.rsqrt(ms + EPS) * g


def _gelu(x):
    return jax.nn.gelu(x)


def _silu(x):
    return x * jax.nn.sigmoid(x)


def _round_weight(w32_ref, w_ref):
    for k0 in range(0, w_ref.shape[0], WEIGHT_CAST_ROWS):
        rows = slice(k0, min(k0 + WEIGHT_CAST_ROWS, w_ref.shape[0]))
        w_ref[rows, :] = w32_ref[rows, :].astype(BF16)


def _norm_matmul_kernel(x_ref, g_ref, w_ref, o_ref, xn_ref, *, row_chunk):
    @pl.when(pl.program_id(1) == 0)
    def _():
        def body(r, carry):
            rows = pl.ds(pl.multiple_of(r * row_chunk, row_chunk), row_chunk)
            xn_ref[rows, :] = _rms_rows(x_ref[rows, :], g_ref[...]).astype(BF16)
            return carry
        lax.fori_loop(0, x_ref.shape[0] // row_chunk, body, 0)

    o_ref[...] = jnp.dot(xn_ref[...], w_ref[...], preferred_element_type=F32)


def _norm_matmul(x, g, w, *, tm, tn, name):
    m, k = x.shape
    n = w.shape[1]
    row_chunk = min(tm, 32)
    return pl.pallas_call(
        functools.partial(_norm_matmul_kernel, row_chunk=row_chunk),
        grid=(m // tm, n // tn),
        in_specs=[
            pl.BlockSpec((tm, k), lambda i, j: (i, 0)),
            pl.BlockSpec((1, k), lambda i, j: (0, 0)),
            pl.BlockSpec((k, tn), lambda i, j: (0, j)),
        ],
        out_specs=pl.BlockSpec((tm, tn), lambda i, j: (i, j)),
        out_shape=jax.ShapeDtypeStruct((m, n), F32),
        scratch_shapes=[pltpu.VMEM((tm, k), BF16)],
        compiler_params=_params(("arbitrary", "arbitrary")),
        name=name,
    )(x, g, w)


def _rope(x, cosf, sinf):
    return x * cosf + pltpu.roll(x, HEAD_DIM // 2, 1) * sinf


def _layernorm_rows(x, g, b):
    mu = jnp.mean(x, axis=-1, keepdims=True)
    xc = x - mu
    var = jnp.mean(xc * xc, axis=-1, keepdims=True)
    return xc * lax.rsqrt(var + EPS) * g + b


def _mixer_chunk(z_ref, r0, cosf, sinf, dmask_ref, qdec_ref, kdec_ref, cdec_ref, lng_ref, lnb_ref,
                 wm_ref, bsf_ref, mix_ref, out_r0, s_ref):
    zr = slice(r0, r0 + CHUNK)
    orows = slice(out_r0, out_r0 + CHUNK)
    for h in range(RET_HEADS):
        q = _rope(z_ref[zr, OFF_Q + h * HEAD_DIM:OFF_Q + (h + 1) * HEAD_DIM], cosf, sinf)
        k = _rope(z_ref[zr, OFF_K + h * HEAD_DIM:OFF_K + (h + 1) * HEAD_DIM], cosf, sinf) * K_SCALE
        v = z_ref[zr, OFF_V + h * HEAD_DIM:OFF_V + (h + 1) * HEAD_DIM].astype(BF16)
        g = z_ref[zr, OFF_G + h * HEAD_DIM:OFF_G + (h + 1) * HEAD_DIM]
        s_old = s_ref[0, h]
        sc = lax.dot_general(q.astype(BF16), k.astype(BF16), (((1,), (1,)), ((), ())),
                             preferred_element_type=F32) * dmask_ref[h]
        o = (jnp.dot(sc.astype(BF16), v, preferred_element_type=F32)
             + jnp.dot((q * qdec_ref[h]).astype(BF16), s_old.astype(BF16),
                       preferred_element_type=F32))
        s_ref[0, h] = s_old * cdec_ref[h] + lax.dot_general(
            (k * kdec_ref[h]).astype(BF16), v, (((0,), (0,)), ((), ())),
            preferred_element_type=F32)
        o = o * lax.rsqrt(jnp.mean(o * o, axis=-1, keepdims=True) + EPS)
        mix_ref[orows, h * HEAD_DIM:(h + 1) * HEAD_DIM] = (o * _silu(g)).astype(mix_ref.dtype)

    vn = _layernorm_rows(_gelu(z_ref[zr, OFF_VG:OFF_VG + GM_WIDTH]), lng_ref[...], lnb_ref[...])
    for grp in range(GM_WIDTH // HEAD_DIM):
        cols = slice(grp * HEAD_DIM, (grp + 1) * HEAD_DIM)
        sp = jnp.dot(wm_ref[grp], vn[:, cols].astype(BF16), preferred_element_type=F32) + bsf_ref[:, cols]
        u = _gelu(z_ref[zr, OFF_U + grp * HEAD_DIM:OFF_U + (grp + 1) * HEAD_DIM])
        mix_ref[orows, RET_QK + grp * HEAD_DIM:RET_QK + (grp + 1) * HEAD_DIM] = (
            (u * sp).astype(mix_ref.dtype))


def _in_mixer_kernel(x_ref, g_ref, w_ref, cos_ref, sin_ref, dmask_ref, qdec_ref, kdec_ref, cdec_ref,
                     lng_ref, lnb_ref, ws_ref, bsf_ref, mix_ref, s_ref,
                     xn_a, xn_b, z_a, z_b, wm_ref, *, steps_per_seq, row_chunk):
    step = pl.program_id(0)

    @pl.when(step % steps_per_seq == 0)
    def _():
        s_ref[...] = jnp.zeros_like(s_ref)

    @pl.when(step == 0)
    def _():
        row = lax.broadcasted_iota(jnp.int32, (CHUNK, CHUNK), 0)
        col = lax.broadcasted_iota(jnp.int32, (CHUNK, CHUNK), 1)
        for grp in range(GM_WIDTH // HEAD_DIM):
            wm_ref[grp] = jnp.where(row >= col, ws_ref[grp], 0.0).astype(BF16)

    half = x_ref.shape[0] // 2
    for xn_ref, z_ref, base in ((xn_a, z_a, 0), (xn_b, z_b, half)):
        for r0 in range(0, half, row_chunk):
            xn_ref[r0:r0 + row_chunk, :] = _rms_rows(
                x_ref[base + r0:base + r0 + row_chunk, :], g_ref[...]).astype(BF16)
        z_ref[...] = jnp.dot(xn_ref[...], w_ref[...], preferred_element_type=F32)
    for z_ref, base in ((z_a, 0), (z_b, half)):
        for c0 in range(0, half, CHUNK):
            pos = slice(base + c0, base + c0 + CHUNK)
            _mixer_chunk(z_ref, c0, cos_ref[pos, :], sin_ref[pos, :], dmask_ref, qdec_ref, kdec_ref,
                         cdec_ref, lng_ref, lnb_ref, wm_ref, bsf_ref, mix_ref, base + c0, s_ref)


def _in_mixer(x, g, w, batch, seq, cosf, sinf, dmask, qdec, kdec, cdec, lng, lnb, ws, bsf, *, tm):
    m, d = x.shape
    n = w.shape[1]
    steps_per_seq = seq // tm
    n_groups = GM_WIDTH // HEAD_DIM
    once = pl.Buffered(1)
    const3 = lambda s: (0, 0, 0)
    const2 = lambda s: (0, 0)
    return pl.pallas_call(
        functools.partial(_in_mixer_kernel, steps_per_seq=steps_per_seq, row_chunk=32),
        grid=(m // tm,),
        in_specs=[
            pl.BlockSpec((tm, d), lambda s: (s, 0)),
            pl.BlockSpec((1, d), const2, pipeline_mode=once),
            pl.BlockSpec((d, n), const2, pipeline_mode=once),
            pl.BlockSpec((tm, HEAD_DIM), lambda s: (s % steps_per_seq, 0)),
            pl.BlockSpec((tm, HEAD_DIM), lambda s: (s % steps_per_seq, 0)),
            pl.BlockSpec((RET_HEADS, CHUNK, CHUNK), const3, pipeline_mode=once),
            pl.BlockSpec((RET_HEADS, CHUNK, HEAD_DIM), const3, pipeline_mode=once),
            pl.BlockSpec((RET_HEADS, CHUNK, HEAD_DIM), const3, pipeline_mode=once),
            pl.BlockSpec((RET_HEADS, 1, HEAD_DIM), const3, pipeline_mode=once),
            pl.BlockSpec((1, GM_WIDTH), const2, pipeline_mode=once),
            pl.BlockSpec((1, GM_WIDTH), const2, pipeline_mode=once),
            pl.BlockSpec((n_groups, CHUNK, CHUNK), const3, pipeline_mode=once),
            pl.BlockSpec((CHUNK, GM_WIDTH), const2, pipeline_mode=once),
        ],
        out_specs=[
            pl.BlockSpec((tm, D_MODEL), lambda s: (s, 0)),
            pl.BlockSpec((1, RET_HEADS, HEAD_DIM, HEAD_DIM), lambda s: (s // steps_per_seq, 0, 0, 0)),
        ],
        out_shape=[
            jax.ShapeDtypeStruct((m, D_MODEL), BF16),
            jax.ShapeDtypeStruct((batch, RET_HEADS, HEAD_DIM, HEAD_DIM), F32),
        ],
        scratch_shapes=[
            pltpu.VMEM((tm // 2, d), BF16), pltpu.VMEM((tm // 2, d), BF16),
            pltpu.VMEM((tm // 2, n), F32), pltpu.VMEM((tm // 2, n), F32),
            pltpu.VMEM((n_groups, CHUNK, CHUNK), BF16),
        ],
        compiler_params=pltpu.CompilerParams(dimension_semantics=("arbitrary",),
                                             vmem_limit_bytes=IN_MIXER_VMEM_LIMIT),
        name="prompt_in_mixer",
    )(x, g, w, cosf, sinf, dmask, qdec, kdec, cdec, lng, lnb, ws, bsf)


def _sample_mixer_kernel(z_ref, s_ref, cos_ref, sin_ref, dmask_ref, qdec_ref, kdec_ref, cdec_ref,
                         lng_ref, lnb_ref, w00_ref, bs0_ref, mix_ref, so_ref, vn_ref, *, tb):
    cosf = cos_ref[...]
    sinf = sin_ref[...]
    row = lax.broadcasted_iota(jnp.int32, (HEAD_DIM, HEAD_DIM), 0)
    col = lax.broadcasted_iota(jnp.int32, (HEAD_DIM, HEAD_DIM), 1)
    eye = row == col
    ones = jnp.ones((HEAD_DIM, HEAD_DIM), BF16)

    def lane_broadcast_columns(rows):
        diag = jnp.concatenate(
            [jnp.where(eye, jnp.broadcast_to(rows[i:i + 1, :], (HEAD_DIM, HEAD_DIM)), 0.0)
             for i in range(rows.shape[0])], axis=0)
        hi = diag.astype(BF16)
        lo = (diag - hi.astype(F32)).astype(BF16)
        return (jnp.dot(hi, ones, preferred_element_type=F32)
                + jnp.dot(lo, ones, preferred_element_type=F32))

    for h in range(RET_HEADS):
        cols = slice(h * HEAD_DIM, (h + 1) * HEAD_DIM)
        q = _rope(z_ref[:, OFF_Q + h * HEAD_DIM:OFF_Q + (h + 1) * HEAD_DIM], cosf, sinf)
        k = _rope(z_ref[:, OFF_K + h * HEAD_DIM:OFF_K + (h + 1) * HEAD_DIM], cosf, sinf) * K_SCALE
        v = z_ref[:, OFF_V + h * HEAD_DIM:OFF_V + (h + 1) * HEAD_DIM]
        g = z_ref[:, OFF_G + h * HEAD_DIM:OFF_G + (h + 1) * HEAD_DIM]
        qdec = qdec_ref[h]
        kdec = kdec_ref[h]
        cdec = cdec_ref[h]
        sc = jnp.sum(q * k, axis=-1, keepdims=True) * dmask_ref[h]
        qk_cols = lane_broadcast_columns(jnp.concatenate([q * qdec, k * kdec], axis=0))
        o_rows = []
        for b in range(tb):
            s_old = s_ref[b, h]
            q_col = qk_cols[b * HEAD_DIM:(b + 1) * HEAD_DIM, :]
            k_col = qk_cols[(tb + b) * HEAD_DIM:(tb + b + 1) * HEAD_DIM, :]
            v_row = v[b:b + 1, :]
            o_rows.append(jnp.sum(q_col * s_old, axis=0, keepdims=True))
            so_ref[b, h] = s_old * cdec + k_col * v_row
        o = sc * v + jnp.concatenate(o_rows, axis=0)
        o = o * lax.rsqrt(jnp.mean(o * o, axis=-1, keepdims=True) + EPS)
        mix_ref[:, cols] = o * _silu(g)

    vn = _layernorm_rows(_gelu(z_ref[:, OFF_VG:OFF_VG + GM_WIDTH]), lng_ref[...], lnb_ref[...])
    vn_ref[...] = vn
    u = _gelu(z_ref[:, OFF_U:OFF_U + GM_WIDTH])
    mix_ref[:, RET_QK:RET_QK + GM_WIDTH] = u * (w00_ref[...] * vn + bs0_ref[...])


def _sample_mixer(z, state, cosf, sinf, dmask, qdec, kdec, cdec, lng, lnb, w00, bs0, *, tb):
    nb = z.shape[0]
    const3 = lambda i: (0, 0, 0)
    const2 = lambda i: (0, 0)
    state_spec = pl.BlockSpec((tb, RET_HEADS, HEAD_DIM, HEAD_DIM), lambda i: (i, 0, 0, 0))
    return pl.pallas_call(
        functools.partial(_sample_mixer_kernel, tb=tb),
        grid=(nb // tb,),
        in_specs=[
            pl.BlockSpec((tb, IN_COLS), lambda i: (i, 0)),
            state_spec,
            pl.BlockSpec((1, HEAD_DIM), const2),
            pl.BlockSpec((1, HEAD_DIM), const2),
            pl.BlockSpec((RET_HEADS, 1, 1), const3),
            pl.BlockSpec((RET_HEADS, 1, HEAD_DIM), const3),
            pl.BlockSpec((RET_HEADS, 1, HEAD_DIM), const3),
            pl.BlockSpec((RET_HEADS, 1, HEAD_DIM), const3),
            pl.BlockSpec((1, GM_WIDTH), const2),
            pl.BlockSpec((1, GM_WIDTH), const2),
            pl.BlockSpec((1, GM_WIDTH), const2),
            pl.BlockSpec((1, GM_WIDTH), const2),
        ],
        out_specs=[
            pl.BlockSpec((tb, D_MODEL), lambda i: (i, 0)),
            state_spec,
            pl.BlockSpec((tb, GM_WIDTH), lambda i: (i, 0)),
        ],
        out_shape=[
            jax.ShapeDtypeStruct((nb, D_MODEL), F32),
            jax.ShapeDtypeStruct(state.shape, F32),
            jax.ShapeDtypeStruct((nb, GM_WIDTH), F32),
        ],
        compiler_params=_params(("arbitrary",)),
        name="sample_mixer",
    )(z, state, cosf, sinf, dmask, qdec, kdec, cdec, lng, lnb, w00, bs0)


def _attn_out_kernel(mix_ref, w32_ref, h_ref, g_ref, mixs_ref, hs_ref,
                     h1_ref, xn_ref, h1s_ref, xns_ref, w_ref, *, row_block, row_chunk):
    @pl.when(pl.program_id(0) == 0)
    def _():
        _round_weight(w32_ref, w_ref)
        h1s = hs_ref[...] + jnp.dot(mixs_ref[...].astype(BF16), w_ref[...],
                                    preferred_element_type=F32)
        h1s_ref[...] = h1s
        xns_ref[...] = _rms_rows(h1s, g_ref[...]).astype(BF16)

    tm = h_ref.shape[0]
    for r_lo in range(0, tm, row_block):
        rows = slice(r_lo, r_lo + row_block)
        h1_ref[rows, :] = h_ref[rows, :] + jnp.dot(mix_ref[rows, :], w_ref[...],
                                                   preferred_element_type=F32)
        for r0 in range(r_lo, r_lo + row_block, row_chunk):
            sub = slice(r0, r0 + row_chunk)
            xn_ref[sub, :] = _rms_rows(h1_ref[sub, :], g_ref[...]).astype(BF16)


def _attn_out(mix, w, h, g, mix_s, h_s, *, tm):
    m, d = h.shape
    nb = h_s.shape[0]
    once = pl.Buffered(1)
    const = lambda i: (0, 0)
    tile = pl.BlockSpec((tm, d), lambda i: (i, 0))
    return pl.pallas_call(
        functools.partial(_attn_out_kernel, row_block=256, row_chunk=32),
        grid=(m // tm,),
        in_specs=[
            tile,
            pl.BlockSpec((d, d), const, pipeline_mode=once),
            tile,
            pl.BlockSpec((1, d), const, pipeline_mode=once),
            pl.BlockSpec((nb, d), const, pipeline_mode=once),
            pl.BlockSpec((nb, d), const, pipeline_mode=once),
        ],
        out_specs=[tile, tile, pl.BlockSpec((nb, d), const), pl.BlockSpec((nb, d), const)],
        out_shape=[
            jax.ShapeDtypeStruct((m, d), F32),
            jax.ShapeDtypeStruct((m, d), BF16),
            jax.ShapeDtypeStruct((nb, d), F32),
            jax.ShapeDtypeStruct((nb, d), BF16),
        ],
        scratch_shapes=[pltpu.VMEM((d, d), BF16)],
        compiler_params=_params(("arbitrary",)),
        name="attn_out",
    )(mix, w, h, g, mix_s, h_s)


CONV_PAD = 8


def _ffn_up_kernel(xn_ref, wg32_ref, wu32_ref, cg_ref, cu_ref, xs_ref, s0g_ref, s0u_ref, s1g_ref,
                   s1u_ref, act_ref, csg_ref, csu_ref, asg_ref, asu_ref, acts_ref,
                   ag_ref, sg_ref, wg_ref, wu_ref, *au_refs, tiles_per_seq, row_chunk, up_splits):
    i = pl.program_id(1)
    tm, tn = act_ref.shape
    first = (i % tiles_per_seq) == 0
    last_rows = up_splits[-1]

    @pl.when(first)
    def _():
        ag_ref[0:CONV_PAD, :] = jnp.zeros((CONV_PAD, tn), F32)
        au_refs[0][0:CONV_PAD, :] = jnp.zeros((CONV_PAD, tn), F32)

    @pl.when(jnp.logical_not(first))
    def _():
        ag_ref[0:CONV_PAD, :] = ag_ref[tm:tm + CONV_PAD, :]
        au_refs[0][0:CONV_PAD, :] = au_refs[-1][last_rows:last_rows + CONV_PAD, :]

    @pl.when(i == 0)
    def _():
        _round_weight(wg32_ref, wg_ref)
        _round_weight(wu32_ref, wu_ref)
        xs = xs_ref[...]
        nb = xs.shape[0]
        a_g = jnp.dot(xs, wg_ref[...], preferred_element_type=F32)
        a_u = jnp.dot(xs, wu_ref[...], preferred_element_type=F32)
        asg_ref[...] = a_g
        asu_ref[...] = a_u

        def conv_s(s0_ref, s1_ref, a, c_ref):
            t = lambda v: v.reshape(nb // 8, 8, tn)
            return (t(s0_ref[...]) * c_ref[0] + t(s1_ref[...]) * c_ref[1] + t(a) * c_ref[2]
                    + c_ref[3]).reshape(nb, tn)
        acts_ref[...] = (_silu(conv_s(s0g_ref, s1g_ref, a_g, cg_ref))
                         * conv_s(s0u_ref, s1u_ref, a_u, cu_ref)).astype(acts_ref.dtype)

    def conv(a_ref, c_ref, r0):
        def window(back):
            lo = r0 + CONV_PAD - back
            return a_ref[lo:lo + row_chunk, :].reshape(row_chunk // 8, 8, tn)
        return window(2) * c_ref[0] + window(1) * c_ref[1] + window(0) * c_ref[2] + c_ref[3]

    ag_ref[CONV_PAD:CONV_PAD + tm, :] = jnp.dot(xn_ref[...], wg_ref[...],
                                                preferred_element_type=F32)
    for r0 in range(0, tm, row_chunk):
        sg_ref[r0:r0 + row_chunk, :] = _silu(conv(ag_ref, cg_ref, r0)).reshape(row_chunk, tn)
    start = 0
    for idx, rows in enumerate(up_splits):
        au_ref = au_refs[idx]
        au_ref[CONV_PAD:CONV_PAD + rows, :] = jnp.dot(xn_ref[start:start + rows, :], wu_ref[...],
                                                      preferred_element_type=F32)
        if idx + 1 < len(up_splits):
            au_refs[idx + 1][0:CONV_PAD, :] = au_ref[rows:rows + CONV_PAD, :]
        start += rows
    start = 0
    for idx, rows in enumerate(up_splits):
        for r0 in range(0, rows, row_chunk):
            up = conv(au_refs[idx], cu_ref, r0).reshape(row_chunk, tn)
            out = slice(start + r0, start + r0 + row_chunk)
            act_ref[out, :] = (sg_ref[out, :] * up).astype(act_ref.dtype)
        start += rows

    @pl.when((i % tiles_per_seq) == tiles_per_seq - 1)
    def _():
        csg_ref[0] = ag_ref[CONV_PAD + tm - 2:CONV_PAD + tm, :]
        csu_ref[0] = au_refs[-1][CONV_PAD + last_rows - 2:CONV_PAD + last_rows, :]


def _ffn_up(xn, w_up, conv_taps, xn_s, conv_state, *, batch, seq, tm, tn):
    m, d = xn.shape
    nb = xn_s.shape[0]
    n_col = D_FF // tn
    tiles_per_seq = seq // tm
    up_splits = (tm // 2, tm // 2)
    sample_cols = lambda k: pl.BlockSpec((nb, tn), lambda j, i: (0, k * n_col + j))
    sample_out = pl.BlockSpec((nb, tn), lambda j, i: (0, j))
    return pl.pallas_call(
        functools.partial(_ffn_up_kernel, tiles_per_seq=tiles_per_seq, row_chunk=64,
                          up_splits=up_splits),
        grid=(n_col, m // tm),
        in_specs=[
            pl.BlockSpec((tm, d), lambda j, i: (i, 0)),
            pl.BlockSpec((d, tn), lambda j, i: (0, j)),
            pl.BlockSpec((d, tn), lambda j, i: (0, n_col + j)),
            pl.BlockSpec((4, 8, tn), lambda j, i: (0, 0, j)),
            pl.BlockSpec((4, 8, tn), lambda j, i: (0, 0, n_col + j)),
            pl.BlockSpec((nb, d), lambda j, i: (0, 0)),
            sample_cols(0), sample_cols(1), sample_cols(2), sample_cols(3),
        ],
        out_specs=[
            pl.BlockSpec((tm, tn), lambda j, i: (i, j)),
            pl.BlockSpec((1, 2, tn), lambda j, i: (i // tiles_per_seq, 0, j)),
            pl.BlockSpec((1, 2, tn), lambda j, i: (i // tiles_per_seq, 0, j)),
            sample_out, sample_out, sample_out,
        ],
        out_shape=[
            jax.ShapeDtypeStruct((m, D_FF), BF16),
            jax.ShapeDtypeStruct((batch, 2, D_FF), F32),
            jax.ShapeDtypeStruct((batch, 2, D_FF), F32),
            jax.ShapeDtypeStruct((nb, D_FF), F32),
            jax.ShapeDtypeStruct((nb, D_FF), F32),
            jax.ShapeDtypeStruct((nb, D_FF), BF16),
        ],
        scratch_shapes=[pltpu.VMEM((tm + CONV_PAD, tn), F32),
                        pltpu.VMEM((tm, tn), F32),
                        pltpu.VMEM((d, tn), BF16),
                        pltpu.VMEM((d, tn), BF16)]
                       + [pltpu.VMEM((rows + CONV_PAD, tn), F32) for rows in up_splits],
        compiler_params=_params(("arbitrary", "arbitrary")),
        name="ffn_up",
    )(xn, w_up, w_up, conv_taps, conv_taps, xn_s, conv_state, conv_state, conv_state, conv_state)


def _ffn_down_kernel(act_ref, w32_ref, h_ref, acts_ref, hs_ref, o_ref, os_ref, w_ref):
    @pl.when(pl.program_id(1) == 0)
    def _():
        _round_weight(w32_ref, w_ref)
        os_ref[...] = hs_ref[...] + jnp.dot(acts_ref[...], w_ref[...], preferred_element_type=F32)

    o_ref[...] = h_ref[...] + jnp.dot(act_ref[...], w_ref[...], preferred_element_type=F32)


def _ffn_down(act, w_down, h1, act_s, h1_s, *, tm, tn):
    m, k = act.shape
    n = w_down.shape[1]
    nb = act_s.shape[0]
    return pl.pallas_call(
        _ffn_down_kernel,
        grid=(n // tn, m // tm),
        in_specs=[
            pl.BlockSpec((tm, k), lambda j, i: (i, 0)),
            pl.BlockSpec((k, tn), lambda j, i: (0, j)),
            pl.BlockSpec((tm, tn), lambda j, i: (i, j)),
            pl.BlockSpec((nb, k), lambda j, i: (0, 0)),
            pl.BlockSpec((nb, tn), lambda j, i: (0, j)),
        ],
        out_specs=[
            pl.BlockSpec((tm, tn), lambda j, i: (i, j)),
            pl.BlockSpec((nb, tn), lambda j, i: (0, j)),
        ],
        out_shape=[
            jax.ShapeDtypeStruct((m, n), F32),
            jax.ShapeDtypeStruct((nb, n), F32),
        ],
        scratch_shapes=[pltpu.VMEM((k, tn), BF16)],
        compiler_params=_params(("arbitrary", "arbitrary")),
        name="ffn_down",
    )(act, w_down, h1, act_s, h1_s)


def _ple_rows(h_ref, p_ref, gple_ref, wg_ref, wp_ref, gfin_ref, y_ref, xn_ref, r_lo, n_rows,
              row_chunk):
    rows = slice(r_lo, r_lo + n_rows)
    for r0 in range(r_lo, r_lo + n_rows, row_chunk):
        sub = slice(r0, r0 + row_chunk)
        xn_ref[sub, :] = _rms_rows(h_ref[sub, :], gple_ref[...]).astype(BF16)
    gate = jax.nn.sigmoid(jnp.dot(xn_ref[rows, :], wg_ref[...], preferred_element_type=F32))
    proj = jnp.dot(p_ref[rows, :].astype(BF16), wp_ref[...], preferred_element_type=F32)
    y_ref[rows, :] = h_ref[rows, :] + proj * gate
    for r0 in range(r_lo, r_lo + n_rows, row_chunk):
        sub = slice(r0, r0 + row_chunk)
        y_ref[sub, :] = _rms_rows(y_ref[sub, :], gfin_ref[...])


def _ple_final_kernel(h_ref, p_ref, gple_ref, wg32_ref, wp32_ref, gfin_ref, hs_ref, ps_ref,
                      y_ref, ys_ref, xn_ref, wg_ref, wp_ref, *, row_block, row_chunk):
    @pl.when(pl.program_id(0) == 0)
    def _():
        _round_weight(wg32_ref, wg_ref)
        _round_weight(wp32_ref, wp_ref)
        _ple_rows(hs_ref, ps_ref, gple_ref, wg_ref, wp_ref, gfin_ref, ys_ref, xn_ref, 0,
                  hs_ref.shape[0], row_chunk)

    for r_lo in range(0, h_ref.shape[0], row_block):
        _ple_rows(h_ref, p_ref, gple_ref, wg_ref, wp_ref, gfin_ref, y_ref, xn_ref, r_lo, row_block,
                  row_chunk)


def _ple_final(h, p, g_ple, w_gate, w_proj, g_final, h_s, p_s, *, tm):
    m, d = h.shape
    pd = p.shape[1]
    nb = h_s.shape[0]
    assert nb <= tm
    once = pl.Buffered(1)
    const = lambda i: (0, 0)
    return pl.pallas_call(
        functools.partial(_ple_final_kernel, row_block=256, row_chunk=32),
        grid=(m // tm,),
        in_specs=[
            pl.BlockSpec((tm, d), lambda i: (i, 0)),
            pl.BlockSpec((tm, pd), lambda i: (i, 0)),
            pl.BlockSpec((1, d), const, pipeline_mode=once),
            pl.BlockSpec((d, d), const, pipeline_mode=once),
            pl.BlockSpec((pd, d), const, pipeline_mode=once),
            pl.BlockSpec((1, d), const, pipeline_mode=once),
            pl.BlockSpec((nb, d), const, pipeline_mode=once),
            pl.BlockSpec((nb, pd), const, pipeline_mode=once),
        ],
        out_specs=[pl.BlockSpec((tm, d), lambda i: (i, 0)), pl.BlockSpec((nb, d), const)],
        out_shape=[jax.ShapeDtypeStruct((m, d), F32), jax.ShapeDtypeStruct((nb, d), F32)],
        scratch_shapes=[pltpu.VMEM((tm, d), BF16), pltpu.VMEM((d, d), BF16),
                        pltpu.VMEM((pd, d), BF16)],
        compiler_params=_params(("arbitrary",)),
        name="ple_final",
    )(h, p, g_ple, w_gate, w_proj, g_final, h_s, p_s)


def _rope_tables(pos):
    inv = ROPE_THETA ** (-np.arange(0, HEAD_DIM, 2, dtype=np.float64) / HEAD_DIM)
    ang = np.asarray(pos, np.float64)[:, None] * inv[None, :]
    cos, sin = np.cos(ang), np.sin(ang)
    return (jnp.asarray(np.concatenate([cos, cos], axis=-1), F32),
            jnp.asarray(np.concatenate([-sin, sin], axis=-1), F32))


def _decay_tables(c):
    log_g = np.log1p(-np.exp2(-5.0 - np.arange(RET_HEADS, dtype=np.float64)))
    idx = np.arange(c, dtype=np.float64)
    diff = idx[:, None] - idx[None, :]
    dmask = np.where(diff >= 0, np.exp(log_g[:, None, None] * np.maximum(diff, 0.0)), 0.0)
    q_dec = np.exp(log_g[:, None] * (idx + 1.0))[..., None]
    k_dec = np.exp(log_g[:, None] * (c - 1.0 - idx))[..., None]
    c_dec = np.exp(log_g * c)[:, None, None]
    return tuple(jnp.asarray(t, F32) for t in (dmask, q_dec, k_dec, c_dec))


def kernel(x_prompt, x_sample, p_prompt, p_sample, state_ret, state_conv, g_attn, w_in, gm_ln_g,
           gm_ln_b, gm_ws, gm_bs, w_o, g_ffn, w_up, conv_w, conv_b, w_down, g_ple, w_ple_gate,
           w_ple_proj, g_final):
    batch, seq, d = x_prompt.shape
    nb = x_sample.shape[0]
    assert x_sample.shape[1] == 1 and g_attn.shape[0] == 1

    w_in_b = w_in[0].astype(BF16)

    g_attn2, g_ffn2, g_ple2 = g_attn[0][None], g_ffn[0][None], g_ple[0][None]
    g_fin2 = g_final[None]
    lng, lnb = gm_ln_g[0][None], gm_ln_b[0][None]
    ws, bs = gm_ws[0], gm_bs[0]
    cw, cb = conv_w[0], conv_b[0][None]
    n_groups = GM_WIDTH // HEAD_DIM

    cos_p, sin_p = _rope_tables(np.arange(seq))
    dmask, q_dec, k_dec, c_dec = _decay_tables(CHUNK)
    qdec_p = jnp.broadcast_to(q_dec, (RET_HEADS, CHUNK, HEAD_DIM))
    kdec_p = jnp.broadcast_to(k_dec, (RET_HEADS, CHUNK, HEAD_DIM))
    cdec_p = jnp.broadcast_to(c_dec, (RET_HEADS, 1, HEAD_DIM))
    bsf = jnp.broadcast_to(bs.T[:, :, None], (CHUNK, n_groups, HEAD_DIM)).reshape(CHUNK, GM_WIDTH)
    cos_s, sin_s = _rope_tables(PAST_LEN + np.arange(1))
    dmask1, q_dec1, k_dec1, c_dec1 = _decay_tables(1)
    qdec_s = jnp.broadcast_to(q_dec1, (RET_HEADS, 1, HEAD_DIM))
    kdec_s = jnp.broadcast_to(k_dec1, (RET_HEADS, 1, HEAD_DIM))
    cdec_s = jnp.broadcast_to(c_dec1, (RET_HEADS, 1, HEAD_DIM))
    w00 = jnp.broadcast_to(ws[:, 0, 0][:, None], (n_groups, HEAD_DIM)).reshape(1, GM_WIDTH)
    bs0 = jnp.broadcast_to(bs[:, 0][:, None], (n_groups, HEAD_DIM)).reshape(1, GM_WIDTH)
    conv_taps = jnp.broadcast_to(jnp.concatenate([cw, cb], axis=0)[:, None, :], (4, 8, 2 * D_FF))

    xp = x_prompt.reshape(batch * seq, d)
    xs = x_sample.reshape(nb, d)

    mix_p, ret_p = _in_mixer(xp, g_attn2, w_in_b, batch, seq, cos_p, sin_p, dmask, qdec_p, kdec_p,
                             cdec_p, lng, lnb, ws, bsf, tm=512)
    z_s = _norm_matmul(xs, g_attn2, w_in_b, tm=nb, tn=1024, name="sample_in_proj")
    mix_s, ret_s, vn_s = _sample_mixer(z_s, state_ret[0], cos_s, sin_s, dmask1, qdec_s, kdec_s,
                                       cdec_s, lng, lnb, w00, bs0, tb=8)
    h1_p, xn2_p, h1_s, xn2_s = _attn_out(mix_p, w_o[0], xp, g_ffn2, mix_s, xs, tm=512)

    act_p, csg_p, csu_p, asg_s, asu_s, act_s = _ffn_up(
        xn2_p, w_up[0], conv_taps, xn2_s, state_conv[0].reshape(nb, 2 * 2 * D_FF), batch=batch,
        seq=seq, tm=1024, tn=512)
    h2_p, h2_s = _ffn_down(act_p, w_down[0], h1_p, act_s, h1_s, tm=512, tn=512)

    y_p, y_s = _ple_final(h2_p, p_prompt[0].reshape(batch * seq, PLE_DIM), g_ple2, w_ple_gate[0],
                          w_ple_proj[0], g_fin2, h2_s, p_sample[0].reshape(nb, PLE_DIM), tm=512)

    conv_p = jnp.concatenate([csg_p, csu_p], axis=-1)[None]
    conv_s = jnp.concatenate([state_conv[0][:, 1, :], asg_s, asu_s], axis=-1)
    return (y_p.reshape(batch, seq, d), y_s.reshape(nb, 1, d), ret_p[None], conv_p,
            ret_s[None], conv_s.reshape(1, nb, 2, 2 * D_FF), vn_s.reshape(1, nb, 1, GM_WIDTH))
```

```python
import functools
import math

import jax
import jax.numpy as jnp
import numpy as np
from jax import lax
from jax.experimental import pallas as pl
from jax.experimental.pallas import tpu as pltpu

F32 = jnp.float32
BF16 = jnp.bfloat16

D_MODEL = 2048
RET_HEADS = 8
HEAD_DIM = 128
CHUNK = 128
RET_QK = RET_HEADS * HEAD_DIM
GM_WIDTH = 1024
IN_COLS = 6144
D_FF = 5632
PLE_DIM = 256
ROPE_THETA = 10000.0
PAST_LEN = 16384
EPS = 1e-6
K_SCALE = HEAD_DIM ** -0.5

OFF_Q, OFF_K, OFF_V, OFF_G, OFF_U, OFF_VG = 0, 1024, 2048, 3072, 4096, 5120

VMEM_LIMIT = 56 * 1024 * 1024
LANES = 128
IN_MIXER_VMEM_LIMIT = 62 * 1024 * 1024
WEIGHT_CAST_ROWS = 256


def _params(semantics):
    return pltpu.CompilerParams(dimension_semantics=semantics, vmem_limit_bytes=VMEM_LIMIT)


def _rms_rows(x, g):
    ms = jnp.mean(x * x, axis=-1, keepdims=True)
    return x * lax.rsqrt(ms + EPS) * g


def _gelu(x):
    return jax.nn.gelu(x)


def _silu(x):
    return x * jax.nn.sigmoid(x)


def _round_weight(w32_ref, w_ref):
    for k0 in range(0, w_ref.shape[0], WEIGHT_CAST_ROWS):
        rows = slice(k0, min(k0 + WEIGHT_CAST_ROWS, w_ref.shape[0]))
        w_ref[rows, :] = w32_ref[rows, :].astype(BF16)


def _norm_matmul_kernel(x_ref, g_ref, w_ref, o_ref, xn_ref, *, row_chunk):
    @pl.when(pl.program_id(1) == 0)
    def _():
        def body(r, carry):
            rows = pl.ds(pl.multiple_of(r * row_chunk, row_chunk), row_chunk)
            xn_ref[rows, :] = _rms_rows(x_ref[rows, :], g_ref[...]).astype(BF16)
            return carry
        lax.fori_loop(0, x_ref.shape[0] // row_chunk, body, 0)

    o_ref[...] = jnp.dot(xn_ref[...], w_ref[...], preferred_element_type=F32)


def _norm_matmul(x, g, w, *, tm, tn, name):
    m, k = x.shape
    n = w.shape[1]
    row_chunk = min(tm, 32)
    return pl.pallas_call(
        functools.partial(_norm_matmul_kernel, row_chunk=row_chunk),
        grid=(m // tm, n // tn),
        in_specs=[
            pl.BlockSpec((tm, k), lambda i, j: (i, 0)),
            pl.BlockSpec((1, k), lambda i, j: (0, 0)),
            pl.BlockSpec((k, tn), lambda i, j: (0, j)),
        ],
        out_specs=pl.BlockSpec((tm, tn), lambda i, j: (i, j)),
        out_shape=jax.ShapeDtypeStruct((m, n), F32),
        scratch_shapes=[pltpu.VMEM((tm, k), BF16)],
        compiler_params=_params(("arbitrary", "arbitrary")),
        name=name,
    )(x, g, w)


def _rope(x, cosf, sinf):
    return x * cosf + pltpu.roll(x, HEAD_DIM // 2, 1) * sinf


def _layernorm_rows(x, g, b):
    mu = jnp.mean(x, axis=-1, keepdims=True)
    xc = x - mu
    var = jnp.mean(xc * xc, axis=-1, keepdims=True)
    return xc * lax.rsqrt(var + EPS) * g + b


def _retention_heads(z_ref, r0, heads, cosf, sinf, dmask_ref, qdec_ref, kdec_ref, cdec_ref,
                     mix_ref, out_r0, state_refs):
    zr = slice(r0, r0 + CHUNK)
    orows = slice(out_r0, out_r0 + CHUNK)
    staged = []
    for h in heads:
        q = _rope(z_ref[zr, OFF_Q + h * HEAD_DIM:OFF_Q + (h + 1) * HEAD_DIM], cosf, sinf)
        k = _rope(z_ref[zr, OFF_K + h * HEAD_DIM:OFF_K + (h + 1) * HEAD_DIM], cosf, sinf) * K_SCALE
        v = z_ref[zr, OFF_V + h * HEAD_DIM:OFF_V + (h + 1) * HEAD_DIM].astype(BF16)
        s_old = state_refs[h][...]
        sc = lax.dot_general(q.astype(BF16), k.astype(BF16), (((1,), (1,)), ((), ())),
                             preferred_element_type=F32) * dmask_ref[h]
        o_cross = jnp.dot((q * qdec_ref[h]).astype(BF16), s_old.astype(BF16),
                          preferred_element_type=F32)
        state_refs[h][...] = s_old * cdec_ref[h] + lax.dot_general(
            (k * kdec_ref[h]).astype(BF16), v, (((0,), (0,)), ((), ())),
            preferred_element_type=F32)
        staged.append((h, sc, o_cross, v))
    for h, sc, o_cross, v in staged:
        o = jnp.dot(sc.astype(BF16), v, preferred_element_type=F32) + o_cross
        o = o * lax.rsqrt(jnp.mean(o * o, axis=-1, keepdims=True) + EPS)
        g = z_ref[zr, OFF_G + h * HEAD_DIM:OFF_G + (h + 1) * HEAD_DIM]
        mix_ref[orows, h * HEAD_DIM:(h + 1) * HEAD_DIM] = (o * _silu(g)).astype(mix_ref.dtype)


def _gating_norm(z_ref, r0, lng_ref, lnb_ref, vn_ref):
    zr = slice(r0, r0 + CHUNK)
    vn = _layernorm_rows(_gelu(z_ref[zr, OFF_VG:OFF_VG + GM_WIDTH]), lng_ref[...], lnb_ref[...])
    vn_ref[...] = vn.astype(BF16)


def _gating_groups(z_ref, r0, groups, vn_ref, wm_ref, bsf_ref, mix_ref, out_r0):
    zr = slice(r0, r0 + CHUNK)
    orows = slice(out_r0, out_r0 + CHUNK)
    for grp in groups:
        cols = slice(grp * HEAD_DIM, (grp + 1) * HEAD_DIM)
        sp = jnp.dot(wm_ref[grp], vn_ref[:, cols], preferred_element_type=F32) + bsf_ref[:, cols]
        u = _gelu(z_ref[zr, OFF_U + grp * HEAD_DIM:OFF_U + (grp + 1) * HEAD_DIM])
        mix_ref[orows, RET_QK + grp * HEAD_DIM:RET_QK + (grp + 1) * HEAD_DIM] = (
            (u * sp).astype(mix_ref.dtype))


N_SLOTS = 8


def _in_mixer_kernel(x_ref, g_ref, w_ref, cos_ref, sin_ref, dmask_ref, qdec_ref, kdec_ref, cdec_ref,
                     lng_ref, lnb_ref, ws_ref, bsf_ref, mix_ref, s_ref,
                     xn_a, xn_b, z_a, z_b, wm_ref, vn_0, vn_1, *state_refs,
                     steps_per_seq, row_chunk):
    step = pl.program_id(0)
    n_groups = GM_WIDTH // HEAD_DIM

    @pl.when(step % steps_per_seq == 0)
    def _():
        for state_ref in state_refs:
            state_ref[...] = jnp.zeros_like(state_ref)

    @pl.when(step == 0)
    def _():
        row = lax.broadcasted_iota(jnp.int32, (CHUNK, CHUNK), 0)
        col = lax.broadcasted_iota(jnp.int32, (CHUNK, CHUNK), 1)
        for grp in range(n_groups):
            wm_ref[grp] = jnp.where(row >= col, ws_ref[grp], 0.0).astype(BF16)

    half = x_ref.shape[0] // 2
    assert half == 2 * CHUNK and RET_HEADS == N_SLOTS and n_groups == N_SLOTS

    def norm_rows(xn_ref, base):
        for r0 in range(0, half, row_chunk):
            xn_ref[r0:r0 + row_chunk, :] = _rms_rows(
                x_ref[base + r0:base + r0 + row_chunk, :], g_ref[...]).astype(BF16)

    def heads(z_ref, c0, base, hs):
        pos = slice(base + c0, base + c0 + CHUNK)
        _retention_heads(z_ref, c0, hs, cos_ref[pos, :], sin_ref[pos, :], dmask_ref, qdec_ref,
                         kdec_ref, cdec_ref, mix_ref, base + c0, state_refs)

    def groups(z_ref, c0, base, vn_ref, gs):
        _gating_groups(z_ref, c0, gs, vn_ref, wm_ref, bsf_ref, mix_ref, base + c0)

    norm_rows(xn_a, 0)
    z_a[...] = jnp.dot(xn_a[...], w_ref[...], preferred_element_type=F32)
    norm_rows(xn_b, half)

    piece = w_ref.shape[1] // N_SLOTS
    xnb = xn_b[...]
    for slot in range(N_SLOTS):
        cols = slice(slot * piece, (slot + 1) * piece)
        z_b[:, cols] = jnp.dot(xnb, w_ref[:, cols], preferred_element_type=F32)
        c0 = (slot // 4) * CHUNK
        if slot == 0:
            _gating_norm(z_a, 0, lng_ref, lnb_ref, vn_0)
        if slot == 3:
            _gating_norm(z_a, CHUNK, lng_ref, lnb_ref, vn_1)
        heads(z_a, c0, 0, (2 * (slot % 4), 2 * (slot % 4) + 1))
        if slot in (1, 2):
            groups(z_a, 0, 0, vn_0, range(4 * (slot - 1), 4 * slot))
        if slot in (4, 5):
            groups(z_a, CHUNK, 0, vn_1, range(4 * (slot - 4), 4 * (slot - 3)))

    for c0, vn_ref in ((0, vn_0), (CHUNK, vn_1)):
        _gating_norm(z_b, c0, lng_ref, lnb_ref, vn_ref)
        heads(z_b, c0, half, range(RET_HEADS))
        groups(z_b, c0, half, vn_ref, range(n_groups))
    for h, state_ref in enumerate(state_refs):
        s_ref[0, h] = state_ref[...]


def _in_mixer(x, g, w, batch, seq, cosf, sinf, dmask, qdec, kdec, cdec, lng, lnb, ws, bsf, *, tm):
    m, d = x.shape
    n = w.shape[1]
    steps_per_seq = seq // tm
    n_groups = GM_WIDTH // HEAD_DIM
    once = pl.Buffered(1)
    const3 = lambda s: (0, 0, 0)
    const2 = lambda s: (0, 0)
    return pl.pallas_call(
        functools.partial(_in_mixer_kernel, steps_per_seq=steps_per_seq, row_chunk=32),
        grid=(m // tm,),
        in_specs=[
            pl.BlockSpec((tm, d), lambda s: (s, 0)),
            pl.BlockSpec((1, d), const2, pipeline_mode=once),
            pl.BlockSpec((d, n), const2, pipeline_mode=once),
            pl.BlockSpec((tm, HEAD_DIM), lambda s: (s % steps_per_seq, 0)),
            pl.BlockSpec((tm, HEAD_DIM), lambda s: (s % steps_per_seq, 0)),
            pl.BlockSpec((RET_HEADS, CHUNK, CHUNK), const3, pipeline_mode=once),
            pl.BlockSpec((RET_HEADS, CHUNK, HEAD_DIM), const3, pipeline_mode=once),
            pl.BlockSpec((RET_HEADS, CHUNK, HEAD_DIM), const3, pipeline_mode=once),
            pl.BlockSpec((RET_HEADS, 1, HEAD_DIM), const3, pipeline_mode=once),
            pl.BlockSpec((1, GM_WIDTH), const2, pipeline_mode=once),
            pl.BlockSpec((1, GM_WIDTH), const2, pipeline_mode=once),
            pl.BlockSpec((n_groups, CHUNK, CHUNK), const3, pipeline_mode=once),
            pl.BlockSpec((CHUNK, GM_WIDTH), const2, pipeline_mode=once),
        ],
        out_specs=[
            pl.BlockSpec((tm, D_MODEL), lambda s: (s, 0)),
            pl.BlockSpec((1, RET_HEADS, HEAD_DIM, HEAD_DIM), lambda s: (s // steps_per_seq, 0, 0, 0)),
        ],
        out_shape=[
            jax.ShapeDtypeStruct((m, D_MODEL), BF16),
            jax.ShapeDtypeStruct((batch, RET_HEADS, HEAD_DIM, HEAD_DIM), F32),
        ],
        scratch_shapes=[
            pltpu.VMEM((tm // 2, d), BF16), pltpu.VMEM((tm // 2, d), BF16),
            pltpu.VMEM((tm // 2, n), F32), pltpu.VMEM((tm // 2, n), F32),
            pltpu.VMEM((n_groups, CHUNK, CHUNK), BF16),
            pltpu.VMEM((CHUNK, GM_WIDTH), BF16), pltpu.VMEM((CHUNK, GM_WIDTH), BF16),
        ] + [pltpu.VMEM((HEAD_DIM, HEAD_DIM), F32) for _ in range(RET_HEADS)],
        compiler_params=pltpu.CompilerParams(dimension_semantics=("arbitrary",),
                                             vmem_limit_bytes=IN_MIXER_VMEM_LIMIT),
        name="prompt_in_mixer",
    )(x, g, w, cosf, sinf, dmask, qdec, kdec, cdec, lng, lnb, ws, bsf)


def _sample_mixer_kernel(z_ref, s_ref, cos_ref, sin_ref, dmask_ref, qdec_ref, kdec_ref, cdec_ref,
                         lng_ref, lnb_ref, w00_ref, bs0_ref, mix_ref, so_ref, vn_ref, *, tb):
    cosf = cos_ref[...]
    sinf = sin_ref[...]
    row = lax.broadcasted_iota(jnp.int32, (HEAD_DIM, HEAD_DIM), 0)
    col = lax.broadcasted_iota(jnp.int32, (HEAD_DIM, HEAD_DIM), 1)
    eye = row == col
    ones = jnp.ones((HEAD_DIM, HEAD_DIM), BF16)

    def lane_broadcast_columns(rows):
        diag = jnp.concatenate(
            [jnp.where(eye, jnp.broadcast_to(rows[i:i + 1, :], (HEAD_DIM, HEAD_DIM)), 0.0)
             for i in range(rows.shape[0])], axis=0)
        hi = diag.astype(BF16)
        lo = (diag - hi.astype(F32)).astype(BF16)
        return (jnp.dot(hi, ones, preferred_element_type=F32)
                + jnp.dot(lo, ones, preferred_element_type=F32))

    for h in range(RET_HEADS):
        cols = slice(h * HEAD_DIM, (h + 1) * HEAD_DIM)
        q = _rope(z_ref[:, OFF_Q + h * HEAD_DIM:OFF_Q + (h + 1) * HEAD_DIM], cosf, sinf)
        k = _rope(z_ref[:, OFF_K + h * HEAD_DIM:OFF_K + (h + 1) * HEAD_DIM], cosf, sinf) * K_SCALE
        v = z_ref[:, OFF_V + h * HEAD_DIM:OFF_V + (h + 1) * HEAD_DIM]
        g = z_ref[:, OFF_G + h * HEAD_DIM:OFF_G + (h + 1) * HEAD_DIM]
        qdec = qdec_ref[h]
        kdec = kdec_ref[h]
        cdec = cdec_ref[h]
        sc = jnp.sum(q * k, axis=-1, keepdims=True) * dmask_ref[h]
        qk_cols = lane_broadcast_columns(jnp.concatenate([q * qdec, k * kdec], axis=0))
        o_rows = []
        for b in range(tb):
            s_old = s_ref[b, h]
            q_col = qk_cols[b * HEAD_DIM:(b + 1) * HEAD_DIM, :]
            k_col = qk_cols[(tb + b) * HEAD_DIM:(tb + b + 1) * HEAD_DIM, :]
            v_row = v[b:b + 1, :]
            o_rows.append(jnp.sum(q_col * s_old, axis=0, keepdims=True))
            so_ref[b, h] = s_old * cdec + k_col * v_row
        o = sc * v + jnp.concatenate(o_rows, axis=0)
        o = o * lax.rsqrt(jnp.mean(o * o, axis=-1, keepdims=True) + EPS)
        mix_ref[:, cols] = o * _silu(g)

    vn = _layernorm_rows(_gelu(z_ref[:, OFF_VG:OFF_VG + GM_WIDTH]), lng_ref[...], lnb_ref[...])
    vn_ref[...] = vn
    u = _gelu(z_ref[:, OFF_U:OFF_U + GM_WIDTH])
    mix_ref[:, RET_QK:RET_QK + GM_WIDTH] = u * (w00_ref[...] * vn + bs0_ref[...])


def _sample_mixer(z, state, cosf, sinf, dmask, qdec, kdec, cdec, lng, lnb, w00, bs0, *, tb):
    nb = z.shape[0]
    const3 = lambda i: (0, 0, 0)
    const2 = lambda i: (0, 0)
    state_spec = pl.BlockSpec((tb, RET_HEADS, HEAD_DIM, HEAD_DIM), lambda i: (i, 0, 0, 0))
    return pl.pallas_call(
        functools.partial(_sample_mixer_kernel, tb=tb),
        grid=(nb // tb,),
        in_specs=[
            pl.BlockSpec((tb, IN_COLS), lambda i: (i, 0)),
            state_spec,
            pl.BlockSpec((1, HEAD_DIM), const2),
            pl.BlockSpec((1, HEAD_DIM), const2),
            pl.BlockSpec((RET_HEADS, 1, 1), const3),
            pl.BlockSpec((RET_HEADS, 1, HEAD_DIM), const3),
            pl.BlockSpec((RET_HEADS, 1, HEAD_DIM), const3),
            pl.BlockSpec((RET_HEADS, 1, HEAD_DIM), const3),
            pl.BlockSpec((1, GM_WIDTH), const2),
            pl.BlockSpec((1, GM_WIDTH), const2),
            pl.BlockSpec((1, GM_WIDTH), const2),
            pl.BlockSpec((1, GM_WIDTH), const2),
        ],
        out_specs=[
            pl.BlockSpec((tb, D_MODEL), lambda i: (i, 0)),
            state_spec,
            pl.BlockSpec((tb, GM_WIDTH), lambda i: (i, 0)),
        ],
        out_shape=[
            jax.ShapeDtypeStruct((nb, D_MODEL), F32),
            jax.ShapeDtypeStruct(state.shape, F32),
            jax.ShapeDtypeStruct((nb, GM_WIDTH), F32),
        ],
        compiler_params=_params(("arbitrary",)),
        name="sample_mixer",
    )(z, state, cosf, sinf, dmask, qdec, kdec, cdec, lng, lnb, w00, bs0)


def _attn_out_kernel(mix_ref, w32_ref, h_ref, g_ref, mixs_ref, hs_ref,
                     h1_ref, xn_ref, h1s_ref, xns_ref, w_ref, *, row_block, row_chunk):
    @pl.when(pl.program_id(0) == 0)
    def _():
        _round_weight(w32_ref, w_ref)
        h1s = hs_ref[...] + jnp.dot(mixs_ref[...].astype(BF16), w_ref[...],
                                    preferred_element_type=F32)
        h1s_ref[...] = h1s
        xns_ref[...] = _rms_rows(h1s, g_ref[...]).astype(BF16)

    tm = h_ref.shape[0]
    for r_lo in range(0, tm, row_block):
        rows = slice(r_lo, r_lo + row_block)
        h1_ref[rows, :] = h_ref[rows, :] + jnp.dot(mix_ref[rows, :], w_ref[...],
                                                   preferred_element_type=F32)
        for r0 in range(r_lo, r_lo + row_block, row_chunk):
            sub = slice(r0, r0 + row_chunk)
            xn_ref[sub, :] = _rms_rows(h1_ref[sub, :], g_ref[...]).astype(BF16)


def _attn_out(mix, w, h, g, mix_s, h_s, *, tm):
    m, d = h.shape
    nb = h_s.shape[0]
    once = pl.Buffered(1)
    const = lambda i: (0, 0)
    tile = pl.BlockSpec((tm, d), lambda i: (i, 0))
    return pl.pallas_call(
        functools.partial(_attn_out_kernel, row_block=256, row_chunk=32),
        grid=(m // tm,),
        in_specs=[
            tile,
            pl.BlockSpec((d, d), const, pipeline_mode=once),
            tile,
            pl.BlockSpec((1, d), const, pipeline_mode=once),
            pl.BlockSpec((nb, d), const, pipeline_mode=once),
            pl.BlockSpec((nb, d), const, pipeline_mode=once),
        ],
        out_specs=[tile, tile, pl.BlockSpec((nb, d), const), pl.BlockSpec((nb, d), const)],
        out_shape=[
            jax.ShapeDtypeStruct((m, d), F32),
            jax.ShapeDtypeStruct((m, d), BF16),
            jax.ShapeDtypeStruct((nb, d), F32),
            jax.ShapeDtypeStruct((nb, d), BF16),
        ],
        scratch_shapes=[pltpu.VMEM((d, d), BF16)],
        compiler_params=_params(("arbitrary",)),
        name="attn_out",
    )(mix, w, h, g, mix_s, h_s)


CONV_PAD = 8


def _ffn_up_kernel(xn_ref, wg32_ref, wu32_ref, cg_ref, cu_ref, xs_ref, s0g_ref, s0u_ref, s1g_ref,
                   s1u_ref, act_ref, csg_ref, csu_ref, asg_ref, asu_ref, acts_ref,
                   ag_ref, sg_ref, wg_ref, wu_ref, *au_refs, tiles_per_seq, row_chunk, up_splits):
    i = pl.program_id(1)
    tm, tn = act_ref.shape
    first = (i % tiles_per_seq) == 0
    last_rows = up_splits[-1]

    @pl.when(first)
    def _():
        ag_ref[0:CONV_PAD, :] = jnp.zeros((CONV_PAD, tn), F32)
        au_refs[0][0:CONV_PAD, :] = jnp.zeros((CONV_PAD, tn), F32)

    @pl.when(jnp.logical_not(first))
    def _():
        ag_ref[0:CONV_PAD, :] = ag_ref[tm:tm + CONV_PAD, :]
        au_refs[0][0:CONV_PAD, :] = au_refs[-1][last_rows:last_rows + CONV_PAD, :]

    @pl.when(i == 0)
    def _():
        _round_weight(wg32_ref, wg_ref)
        _round_weight(wu32_ref, wu_ref)
        xs = xs_ref[...]
        nb = xs.shape[0]
        a_g = jnp.dot(xs, wg_ref[...], preferred_element_type=F32)
        a_u = jnp.dot(xs, wu_ref[...], preferred_element_type=F32)
        asg_ref[...] = a_g
        asu_ref[...] = a_u

        def conv_s(s0_ref, s1_ref, a, c_ref):
            t = lambda v: v.reshape(nb // 8, 8, tn)
            return (t(s0_ref[...]) * c_ref[0] + t(s1_ref[...]) * c_ref[1] + t(a) * c_ref[2]
                    + c_ref[3]).reshape(nb, tn)
        acts_ref[...] = (_silu(conv_s(s0g_ref, s1g_ref, a_g, cg_ref))
                         * conv_s(s0u_ref, s1u_ref, a_u, cu_ref)).astype(acts_ref.dtype)

    def conv(a_ref, c_ref, r0):
        def window(back):
            lo = r0 + CONV_PAD - back
            return a_ref[lo:lo + row_chunk, :].reshape(row_chunk // 8, 8, tn)
        return window(2) * c_ref[0] + window(1) * c_ref[1] + window(0) * c_ref[2] + c_ref[3]

    ag_ref[CONV_PAD:CONV_PAD + tm, :] = jnp.dot(xn_ref[...], wg_ref[...],
                                                preferred_element_type=F32)
    for r0 in range(0, tm, row_chunk):
        sg_ref[r0:r0 + row_chunk, :] = _silu(conv(ag_ref, cg_ref, r0)).reshape(row_chunk, tn)
    start = 0
    for idx, rows in enumerate(up_splits):
        au_ref = au_refs[idx]
        au_ref[CONV_PAD:CONV_PAD + rows, :] = jnp.dot(xn_ref[start:start + rows, :], wu_ref[...],
                                                      preferred_element_type=F32)
        if idx + 1 < len(up_splits):
            au_refs[idx + 1][0:CONV_PAD, :] = au_ref[rows:rows + CONV_PAD, :]
        start += rows
    start = 0
    for idx, rows in enumerate(up_splits):
        for r0 in range(0, rows, row_chunk):
            up = conv(au_refs[idx], cu_ref, r0).reshape(row_chunk, tn)
            out = slice(start + r0, start + r0 + row_chunk)
            act_ref[out, :] = (sg_ref[out, :] * up).astype(act_ref.dtype)
        start += rows

    @pl.when((i % tiles_per_seq) == tiles_per_seq - 1)
    def _():
        csg_ref[0] = ag_ref[CONV_PAD + tm - 2:CONV_PAD + tm, :]
        csu_ref[0] = au_refs[-1][CONV_PAD + last_rows - 2:CONV_PAD + last_rows, :]


def _ffn_up(xn, w_up, conv_taps, xn_s, conv_state, *, batch, seq, tm, tn):
    m, d = xn.shape
    nb = xn_s.shape[0]
    n_col = D_FF // tn
    tiles_per_seq = seq // tm
    up_splits = (tm // 2, tm // 2)
    sample_cols = lambda k: pl.BlockSpec((nb, tn), lambda j, i: (0, k * n_col + j))
    sample_out = pl.BlockSpec((nb, tn), lambda j, i: (0, j))
    return pl.pallas_call(
        functools.partial(_ffn_up_kernel, tiles_per_seq=tiles_per_seq, row_chunk=64,
                          up_splits=up_splits),
        grid=(n_col, m // tm),
        in_specs=[
            pl.BlockSpec((tm, d), lambda j, i: (i, 0)),
            pl.BlockSpec((d, tn), lambda j, i: (0, j)),
            pl.BlockSpec((d, tn), lambda j, i: (0, n_col + j)),
            pl.BlockSpec((4, 8, tn), lambda j, i: (0, 0, j)),
            pl.BlockSpec((4, 8, tn), lambda j, i: (0, 0, n_col + j)),
            pl.BlockSpec((nb, d), lambda j, i: (0, 0)),
            sample_cols(0), sample_cols(1), sample_cols(2), sample_cols(3),
        ],
        out_specs=[
            pl.BlockSpec((tm, tn), lambda j, i: (i, j)),
            pl.BlockSpec((1, 2, tn), lambda j, i: (i // tiles_per_seq, 0, j)),
            pl.BlockSpec((1, 2, tn), lambda j, i: (i // tiles_per_seq, 0, j)),
            sample_out, sample_out, sample_out,
        ],
        out_shape=[
            jax.ShapeDtypeStruct((m, D_FF), BF16),
            jax.ShapeDtypeStruct((batch, 2, D_FF), F32),
            jax.ShapeDtypeStruct((batch, 2, D_FF), F32),
            jax.ShapeDtypeStruct((nb, D_FF), F32),
            jax.ShapeDtypeStruct((nb, D_FF), F32),
            jax.ShapeDtypeStruct((nb, D_FF), BF16),
        ],
        scratch_shapes=[pltpu.VMEM((tm + CONV_PAD, tn), F32),
                        pltpu.VMEM((tm, tn), F32),
                        pltpu.VMEM((d, tn), BF16),
                        pltpu.VMEM((d, tn), BF16)]
                       + [pltpu.VMEM((rows + CONV_PAD, tn), F32) for rows in up_splits],
        compiler_params=_params(("arbitrary", "arbitrary")),
        name="ffn_up",
    )(xn, w_up, w_up, conv_taps, conv_taps, xn_s, conv_state, conv_state, conv_state, conv_state)


def _ffn_down_kernel(act_ref, w32_ref, h_ref, acts_ref, hs_ref, o_ref, os_ref, w_ref):
    @pl.when(pl.program_id(1) == 0)
    def _():
        _round_weight(w32_ref, w_ref)
        os_ref[...] = hs_ref[...] + jnp.dot(acts_ref[...], w_ref[...], preferred_element_type=F32)

    o_ref[...] = h_ref[...] + jnp.dot(act_ref[...], w_ref[...], preferred_element_type=F32)


def _ffn_down(act, w_down, h1, act_s, h1_s, *, tm, tn):
    m, k = act.shape
    n = w_down.shape[1]
    nb = act_s.shape[0]
    return pl.pallas_call(
        _ffn_down_kernel,
        grid=(n // tn, m // tm),
        in_specs=[
            pl.BlockSpec((tm, k), lambda j, i: (i, 0)),
            pl.BlockSpec((k, tn), lambda j, i: (0, j)),
            pl.BlockSpec((tm, tn), lambda j, i: (i, j)),
            pl.BlockSpec((nb, k), lambda j, i: (0, 0)),
            pl.BlockSpec((nb, tn), lambda j, i: (0, j)),
        ],
        out_specs=[
            pl.BlockSpec((tm, tn), lambda j, i: (i, j)),
            pl.BlockSpec((nb, tn), lambda j, i: (0, j)),
        ],
        out_shape=[
            jax.ShapeDtypeStruct((m, n), F32),
            jax.ShapeDtypeStruct((nb, n), F32),
        ],
        scratch_shapes=[pltpu.VMEM((k, tn), BF16)],
        compiler_params=_params(("arbitrary", "arbitrary")),
        name="ffn_down",
    )(act, w_down, h1, act_s, h1_s)


def _ple_rows(h_ref, p_ref, gple_ref, wg_ref, wp_ref, gfin_ref, y_ref, xn_ref, r_lo, n_rows,
              row_chunk):
    rows = slice(r_lo, r_lo + n_rows)
    for r0 in range(r_lo, r_lo + n_rows, row_chunk):
        sub = slice(r0, r0 + row_chunk)
        xn_ref[sub, :] = _rms_rows(h_ref[sub, :], gple_ref[...]).astype(BF16)
    gate = jax.nn.sigmoid(jnp.dot(xn_ref[rows, :], wg_ref[...], preferred_element_type=F32))
    proj = jnp.dot(p_ref[rows, :].astype(BF16), wp_ref[...], preferred_element_type=F32)
    y_ref[rows, :] = h_ref[rows, :] + proj * gate
    for r0 in range(r_lo, r_lo + n_rows, row_chunk):
        sub = slice(r0, r0 + row_chunk)
        y_ref[sub, :] = _rms_rows(y_ref[sub, :], gfin_ref[...])


def _ple_final_kernel(h_ref, p_ref, gple_ref, wg32_ref, wp32_ref, gfin_ref, hs_ref, ps_ref,
                      y_ref, ys_ref, xn_ref, wg_ref, wp_ref, *, row_block, row_chunk):
    @pl.when(pl.program_id(0) == 0)
    def _():
        _round_weight(wg32_ref, wg_ref)
        _round_weight(wp32_ref, wp_ref)
        _ple_rows(hs_ref, ps_ref, gple_ref, wg_ref, wp_ref, gfin_ref, ys_ref, xn_ref, 0,
                  hs_ref.shape[0], row_chunk)

    for r_lo in range(0, h_ref.shape[0], row_block):
        _ple_rows(h_ref, p_ref, gple_ref, wg_ref, wp_ref, gfin_ref, y_ref, xn_ref, r_lo, row_block,
                  row_chunk)


def _ple_final(h, p, g_ple, w_gate, w_proj, g_final, h_s, p_s, *, tm):
    m, d = h.shape
    pd = p.shape[1]
    nb = h_s.shape[0]
    assert nb <= tm
    once = pl.Buffered(1)
    const = lambda i: (0, 0)
    return pl.pallas_call(
        functools.partial(_ple_final_kernel, row_block=256, row_chunk=32),
        grid=(m // tm,),
        in_specs=[
            pl.BlockSpec((tm, d), lambda i: (i, 0)),
            pl.BlockSpec((tm, pd), lambda i: (i, 0)),
            pl.BlockSpec((1, d), const, pipeline_mode=once),
            pl.BlockSpec((d, d), const, pipeline_mode=once),
            pl.BlockSpec((pd, d), const, pipeline_mode=once),
            pl.BlockSpec((1, d), const, pipeline_mode=once),
            pl.BlockSpec((nb, d), const, pipeline_mode=once),
            pl.BlockSpec((nb, pd), const, pipeline_mode=once),
        ],
        out_specs=[pl.BlockSpec((tm, d), lambda i: (i, 0)), pl.BlockSpec((nb, d), const)],
        out_shape=[jax.ShapeDtypeStruct((m, d), F32), jax.ShapeDtypeStruct((nb, d), F32)],
        scratch_shapes=[pltpu.VMEM((tm, d), BF16), pltpu.VMEM((d, d), BF16),
                        pltpu.VMEM((pd, d), BF16)],
        compiler_params=_params(("arbitrary",)),
        name="ple_final",
    )(h, p, g_ple, w_gate, w_proj, g_final, h_s, p_s)


def _rope_tables(pos):
    inv = ROPE_THETA ** (-np.arange(0, HEAD_DIM, 2, dtype=np.float64) / HEAD_DIM)
    ang = np.asarray(pos, np.float64)[:, None] * inv[None, :]
    cos, sin = np.cos(ang), np.sin(ang)
    return (jnp.asarray(np.concatenate([cos, cos], axis=-1), F32),
            jnp.asarray(np.concatenate([-sin, sin], axis=-1), F32))


def _decay_tables(c):
    log_g = np.log1p(-np.exp2(-5.0 - np.arange(RET_HEADS, dtype=np.float64)))
    idx = np.arange(c, dtype=np.float64)
    diff = idx[:, None] - idx[None, :]
    dmask = np.where(diff >= 0, np.exp(log_g[:, None, None] * np.maximum(diff, 0.0)), 0.0)
    q_dec = np.exp(log_g[:, None] * (idx + 1.0))[..., None]
    k_dec = np.exp(log_g[:, None] * (c - 1.0 - idx))[..., None]
    c_dec = np.exp(log_g * c)[:, None, None]
    return tuple(jnp.asarray(t, F32) for t in (dmask, q_dec, k_dec, c_dec))


def kernel(x_prompt, x_sample, p_prompt, p_sample, state_ret, state_conv, g_attn, w_in, gm_ln_g,
           gm_ln_b, gm_ws, gm_bs, w_o, g_ffn, w_up, conv_w, conv_b, w_down, g_ple, w_ple_gate,
           w_ple_proj, g_final):
    batch, seq, d = x_prompt.shape
    nb = x_sample.shape[0]
    assert x_sample.shape[1] == 1 and g_attn.shape[0] == 1

    w_in_b = w_in[0].astype(BF16)

    g_attn2, g_ffn2, g_ple2 = g_attn[0][None], g_ffn[0][None], g_ple[0][None]
    g_fin2 = g_final[None]
    lng, lnb = gm_ln_g[0][None], gm_ln_b[0][None]
    ws, bs = gm_ws[0], gm_bs[0]
    cw, cb = conv_w[0], conv_b[0][None]
    n_groups = GM_WIDTH // HEAD_DIM

    cos_p, sin_p = _rope_tables(np.arange(seq))
    dmask, q_dec, k_dec, c_dec = _decay_tables(CHUNK)
    qdec_p = jnp.broadcast_to(q_dec, (RET_HEADS, CHUNK, HEAD_DIM))
    kdec_p = jnp.broadcast_to(k_dec, (RET_HEADS, CHUNK, HEAD_DIM))
    cdec_p = jnp.broadcast_to(c_dec, (RET_HEADS, 1, HEAD_DIM))
    bsf = jnp.broadcast_to(bs.T[:, :, None], (CHUNK, n_groups, HEAD_DIM)).reshape(CHUNK, GM_WIDTH)
    cos_s, sin_s = _rope_tables(PAST_LEN + np.arange(1))
    dmask1, q_dec1, k_dec1, c_dec1 = _decay_tables(1)
    qdec_s = jnp.broadcast_to(q_dec1, (RET_HEADS, 1, HEAD_DIM))
    kdec_s = jnp.broadcast_to(k_dec1, (RET_HEADS, 1, HEAD_DIM))
    cdec_s = jnp.broadcast_to(c_dec1, (RET_HEADS, 1, HEAD_DIM))
    w00 = jnp.broadcast_to(ws[:, 0, 0][:, None], (n_groups, HEAD_DIM)).reshape(1, GM_WIDTH)
    bs0 = jnp.broadcast_to(bs[:, 0][:, None], (n_groups, HEAD_DIM)).reshape(1, GM_WIDTH)
    conv_taps = jnp.broadcast_to(jnp.concatenate([cw, cb], axis=0)[:, None, :], (4, 8, 2 * D_FF))

    xp = x_prompt.reshape(batch * seq, d)
    xs = x_sample.reshape(nb, d)

    mix_p, ret_p = _in_mixer(xp, g_attn2, w_in_b, batch, seq, cos_p, sin_p, dmask, qdec_p, kdec_p,
                             cdec_p, lng, lnb, ws, bsf, tm=512)
    z_s = _norm_matmul(xs, g_attn2, w_in_b, tm=nb, tn=1024, name="sample_in_proj")
    mix_s, ret_s, vn_s = _sample_mixer(z_s, state_ret[0], cos_s, sin_s, dmask1, qdec_s, kdec_s,
                                       cdec_s, lng, lnb, w00, bs0, tb=8)
    h1_p, xn2_p, h1_s, xn2_s = _attn_out(mix_p, w_o[0], xp, g_ffn2, mix_s, xs, tm=512)

    act_p, csg_p, csu_p, asg_s, asu_s, act_s = _ffn_up(
        xn2_p, w_up[0], conv_taps, xn2_s, state_conv[0].reshape(nb, 2 * 2 * D_FF), batch=batch,
        seq=seq, tm=1024, tn=512)
    h2_p, h2_s = _ffn_down(act_p, w_down[0], h1_p, act_s, h1_s, tm=512, tn=512)

    y_p, y_s = _ple_final(h2_p, p_prompt[0].reshape(batch * seq, PLE_DIM), g_ple2, w_ple_gate[0],
                          w_ple_proj[0], g_fin2, h2_s, p_sample[0].reshape(nb, PLE_DIM), tm=512)

    conv_p = jnp.concatenate([csg_p, csu_p], axis=-1)[None]
    a_s = jnp.concatenate([asg_s, asu_s], axis=-1)
    conv_s = jnp.stack([state_conv[0][:, 1, :], a_s], axis=1)[None]
    return (y_p.reshape(batch, seq, d), y_s.reshape(nb, 1, d), ret_p[None], conv_p,
            ret_s[None], conv_s, vn_s.reshape(1, nb, 1, GM_WIDTH))
```

```python
import functools
import math

import jax
import jax.numpy as jnp
import numpy as np
from jax import lax
from jax.experimental import pallas as pl
from jax.experimental.pallas import tpu as pltpu

F32 = jnp.float32
BF16 = jnp.bfloat16

D_MODEL = 2048
RET_HEADS = 8
HEAD_DIM = 128
CHUNK = 128
RET_QK = RET_HEADS * HEAD_DIM
GM_WIDTH = 1024
IN_COLS = 6144
D_FF = 5632
PLE_DIM = 256
ROPE_THETA = 10000.0
PAST_LEN = 16384
EPS = 1e-6
K_SCALE = HEAD_DIM ** -0.5

OFF_Q, OFF_K, OFF_V, OFF_G, OFF_U, OFF_VG = 0, 1024, 2048, 3072, 4096, 5120

VMEM_LIMIT = 56 * 1024 * 1024
LANES = 128
IN_MIXER_VMEM_LIMIT = 62 * 1024 * 1024
FFN_DOWN_VMEM_LIMIT = 60 * 1024 * 1024
WEIGHT_CAST_ROWS = 256


def _params(semantics):
    return pltpu.CompilerParams(dimension_semantics=semantics, vmem_limit_bytes=VMEM_LIMIT)


def _rms_rows(x, g):
    ms = jnp.mean(x * x, axis=-1, keepdims=True)
    return x * lax.rsqrt(ms + EPS) * g


def _gelu(x):
    return jax.nn.gelu(x)


def _silu(x):
    return x * jax.nn.sigmoid(x)


def _round_weight(w32_ref, w_ref):
    for k0 in range(0, w_ref.shape[0], WEIGHT_CAST_ROWS):
        rows = slice(k0, min(k0 + WEIGHT_CAST_ROWS, w_ref.shape[0]))
        w_ref[rows, :] = w32_ref[rows, :].astype(BF16)


def _norm_matmul_kernel(x_ref, g_ref, w_ref, o_ref, xn_ref, *, row_chunk):
    @pl.when(pl.program_id(1) == 0)
    def _():
        def body(r, carry):
            rows = pl.ds(pl.multiple_of(r * row_chunk, row_chunk), row_chunk)
            xn_ref[rows, :] = _rms_rows(x_ref[rows, :], g_ref[...]).astype(BF16)
            return carry
        lax.fori_loop(0, x_ref.shape[0] // row_chunk, body, 0)

    o_ref[...] = jnp.dot(xn_ref[...], w_ref[...], preferred_element_type=F32)


def _norm_matmul(x, g, w, *, tm, tn, name):
    m, k = x.shape
    n = w.shape[1]
    row_chunk = min(tm, 32)
    return pl.pallas_call(
        functools.partial(_norm_matmul_kernel, row_chunk=row_chunk),
        grid=(m // tm, n // tn),
        in_specs=[
            pl.BlockSpec((tm, k), lambda i, j: (i, 0)),
            pl.BlockSpec((1, k), lambda i, j: (0, 0)),
            pl.BlockSpec((k, tn), lambda i, j: (0, j)),
        ],
        out_specs=pl.BlockSpec((tm, tn), lambda i, j: (i, j)),
        out_shape=jax.ShapeDtypeStruct((m, n), F32),
        scratch_shapes=[pltpu.VMEM((tm, k), BF16)],
        compiler_params=_params(("arbitrary", "arbitrary")),
        name=name,
    )(x, g, w)


def _rope(x, cosf, sinf):
    return x * cosf + pltpu.roll(x, HEAD_DIM // 2, 1) * sinf


def _layernorm_rows(x, g, b):
    mu = jnp.mean(x, axis=-1, keepdims=True)
    xc = x - mu
    var = jnp.mean(xc * xc, axis=-1, keepdims=True)
    return xc * lax.rsqrt(var + EPS) * g + b


def _retention_heads(z_ref, r0, heads, cosf, sinf, dmask_ref, qdec_ref, kdec_ref, cdec_ref,
                     mix_ref, out_r0, state_refs):
    zr = slice(r0, r0 + CHUNK)
    orows = slice(out_r0, out_r0 + CHUNK)
    staged = []
    for h in heads:
        q = _rope(z_ref[zr, OFF_Q + h * HEAD_DIM:OFF_Q + (h + 1) * HEAD_DIM], cosf, sinf)
        k = _rope(z_ref[zr, OFF_K + h * HEAD_DIM:OFF_K + (h + 1) * HEAD_DIM], cosf, sinf) * K_SCALE
        v = z_ref[zr, OFF_V + h * HEAD_DIM:OFF_V + (h + 1) * HEAD_DIM].astype(BF16)
        s_old = state_refs[h][...]
        sc = lax.dot_general(q.astype(BF16), k.astype(BF16), (((1,), (1,)), ((), ())),
                             preferred_element_type=F32) * dmask_ref[h]
        o_cross = jnp.dot((q * qdec_ref[h]).astype(BF16), s_old.astype(BF16),
                          preferred_element_type=F32)
        state_refs[h][...] = s_old * cdec_ref[h] + lax.dot_general(
            (k * kdec_ref[h]).astype(BF16), v, (((0,), (0,)), ((), ())),
            preferred_element_type=F32)
        staged.append((h, sc, o_cross, v))
    for h, sc, o_cross, v in staged:
        o = jnp.dot(sc.astype(BF16), v, preferred_element_type=F32) + o_cross
        o = o * lax.rsqrt(jnp.mean(o * o, axis=-1, keepdims=True) + EPS)
        g = z_ref[zr, OFF_G + h * HEAD_DIM:OFF_G + (h + 1) * HEAD_DIM]
        mix_ref[orows, h * HEAD_DIM:(h + 1) * HEAD_DIM] = (o * _silu(g)).astype(mix_ref.dtype)


def _gating_norm(z_ref, r0, lng_ref, lnb_ref, vn_ref):
    zr = slice(r0, r0 + CHUNK)
    vn = _layernorm_rows(_gelu(z_ref[zr, OFF_VG:OFF_VG + GM_WIDTH]), lng_ref[...], lnb_ref[...])
    vn_ref[...] = vn.astype(BF16)


def _gating_groups(z_ref, r0, groups, vn_ref, wm_ref, bsf_ref, mix_ref, out_r0):
    zr = slice(r0, r0 + CHUNK)
    orows = slice(out_r0, out_r0 + CHUNK)
    for grp in groups:
        cols = slice(grp * HEAD_DIM, (grp + 1) * HEAD_DIM)
        sp = jnp.dot(wm_ref[grp], vn_ref[:, cols], preferred_element_type=F32) + bsf_ref[:, cols]
        u = _gelu(z_ref[zr, OFF_U + grp * HEAD_DIM:OFF_U + (grp + 1) * HEAD_DIM])
        mix_ref[orows, RET_QK + grp * HEAD_DIM:RET_QK + (grp + 1) * HEAD_DIM] = (
            (u * sp).astype(mix_ref.dtype))


N_SLOTS = 8


def _in_mixer_kernel(x_ref, g_ref, w_ref, cos_ref, sin_ref, dmask_ref, qdec_ref, kdec_ref, cdec_ref,
                     lng_ref, lnb_ref, ws_ref, bsf_ref, mix_ref, s_ref,
                     xn_a, xn_b, z_a, z_b, wm_ref, vn_0, vn_1, *state_refs,
                     steps_per_seq, row_chunk):
    step = pl.program_id(0)
    n_groups = GM_WIDTH // HEAD_DIM

    @pl.when(step % steps_per_seq == 0)
    def _():
        for state_ref in state_refs:
            state_ref[...] = jnp.zeros_like(state_ref)

    @pl.when(step == 0)
    def _():
        row = lax.broadcasted_iota(jnp.int32, (CHUNK, CHUNK), 0)
        col = lax.broadcasted_iota(jnp.int32, (CHUNK, CHUNK), 1)
        for grp in range(n_groups):
            wm_ref[grp] = jnp.where(row >= col, ws_ref[grp], 0.0).astype(BF16)

    half = x_ref.shape[0] // 2
    assert half == 2 * CHUNK and RET_HEADS == N_SLOTS and n_groups == N_SLOTS

    def norm_rows(xn_ref, base):
        for r0 in range(0, half, row_chunk):
            xn_ref[r0:r0 + row_chunk, :] = _rms_rows(
                x_ref[base + r0:base + r0 + row_chunk, :], g_ref[...]).astype(BF16)

    def heads(z_ref, c0, base, hs):
        pos = slice(base + c0, base + c0 + CHUNK)
        _retention_heads(z_ref, c0, hs, cos_ref[pos, :], sin_ref[pos, :], dmask_ref, qdec_ref,
                         kdec_ref, cdec_ref, mix_ref, base + c0, state_refs)

    def groups(z_ref, c0, base, vn_ref, gs):
        _gating_groups(z_ref, c0, gs, vn_ref, wm_ref, bsf_ref, mix_ref, base + c0)

    norm_rows(xn_a, 0)
    z_a[...] = jnp.dot(xn_a[...], w_ref[...], preferred_element_type=F32)
    norm_rows(xn_b, half)

    piece = w_ref.shape[1] // N_SLOTS
    xnb = xn_b[...]
    for slot in range(N_SLOTS):
        cols = slice(slot * piece, (slot + 1) * piece)
        z_b[:, cols] = jnp.dot(xnb, w_ref[:, cols], preferred_element_type=F32)
        c0 = (slot // 4) * CHUNK
        if slot == 0:
            _gating_norm(z_a, 0, lng_ref, lnb_ref, vn_0)
        if slot == 3:
            _gating_norm(z_a, CHUNK, lng_ref, lnb_ref, vn_1)
        heads(z_a, c0, 0, (2 * (slot % 4), 2 * (slot % 4) + 1))
        if slot in (1, 2):
            groups(z_a, 0, 0, vn_0, range(4 * (slot - 1), 4 * slot))
        if slot in (4, 5):
            groups(z_a, CHUNK, 0, vn_1, range(4 * (slot - 4), 4 * (slot - 3)))

    for c0, vn_ref in ((0, vn_0), (CHUNK, vn_1)):
        _gating_norm(z_b, c0, lng_ref, lnb_ref, vn_ref)
        heads(z_b, c0, half, range(RET_HEADS))
        groups(z_b, c0, half, vn_ref, range(n_groups))
    for h, state_ref in enumerate(state_refs):
        s_ref[0, h] = state_ref[...]


def _in_mixer(x, g, w, batch, seq, cosf, sinf, dmask, qdec, kdec, cdec, lng, lnb, ws, bsf, *, tm):
    m, d = x.shape
    n = w.shape[1]
    steps_per_seq = seq // tm
    n_groups = GM_WIDTH // HEAD_DIM
    once = pl.Buffered(1)
    const3 = lambda s: (0, 0, 0)
    const2 = lambda s: (0, 0)
    return pl.pallas_call(
        functools.partial(_in_mixer_kernel, steps_per_seq=steps_per_seq, row_chunk=32),
        grid=(m // tm,),
        in_specs=[
            pl.BlockSpec((tm, d), lambda s: (s, 0)),
            pl.BlockSpec((1, d), const2, pipeline_mode=once),
            pl.BlockSpec((d, n), const2, pipeline_mode=once),
            pl.BlockSpec((tm, HEAD_DIM), lambda s: (s % steps_per_seq, 0)),
            pl.BlockSpec((tm, HEAD_DIM), lambda s: (s % steps_per_seq, 0)),
            pl.BlockSpec((RET_HEADS, CHUNK, CHUNK), const3, pipeline_mode=once),
            pl.BlockSpec((RET_HEADS, CHUNK, HEAD_DIM), const3, pipeline_mode=once),
            pl.BlockSpec((RET_HEADS, CHUNK, HEAD_DIM), const3, pipeline_mode=once),
            pl.BlockSpec((RET_HEADS, 1, HEAD_DIM), const3, pipeline_mode=once),
            pl.BlockSpec((1, GM_WIDTH), const2, pipeline_mode=once),
            pl.BlockSpec((1, GM_WIDTH), const2, pipeline_mode=once),
            pl.BlockSpec((n_groups, CHUNK, CHUNK), const3, pipeline_mode=once),
            pl.BlockSpec((CHUNK, GM_WIDTH), const2, pipeline_mode=once),
        ],
        out_specs=[
            pl.BlockSpec((tm, D_MODEL), lambda s: (s, 0)),
            pl.BlockSpec((1, RET_HEADS, HEAD_DIM, HEAD_DIM), lambda s: (s // steps_per_seq, 0, 0, 0)),
        ],
        out_shape=[
            jax.ShapeDtypeStruct((m, D_MODEL), BF16),
            jax.ShapeDtypeStruct((batch, RET_HEADS, HEAD_DIM, HEAD_DIM), F32),
        ],
        scratch_shapes=[
            pltpu.VMEM((tm // 2, d), BF16), pltpu.VMEM((tm // 2, d), BF16),
            pltpu.VMEM((tm // 2, n), F32), pltpu.VMEM((tm // 2, n), F32),
            pltpu.VMEM((n_groups, CHUNK, CHUNK), BF16),
            pltpu.VMEM((CHUNK, GM_WIDTH), BF16), pltpu.VMEM((CHUNK, GM_WIDTH), BF16),
        ] + [pltpu.VMEM((HEAD_DIM, HEAD_DIM), F32) for _ in range(RET_HEADS)],
        compiler_params=pltpu.CompilerParams(dimension_semantics=("arbitrary",),
                                             vmem_limit_bytes=IN_MIXER_VMEM_LIMIT),
        name="prompt_in_mixer",
    )(x, g, w, cosf, sinf, dmask, qdec, kdec, cdec, lng, lnb, ws, bsf)


def _sample_mixer_kernel(z_ref, s_ref, cos_ref, sin_ref, dmask_ref, qdec_ref, kdec_ref, cdec_ref,
                         lng_ref, lnb_ref, w00_ref, bs0_ref, mix_ref, so_ref, vn_ref, *, tb):
    cosf = cos_ref[...]
    sinf = sin_ref[...]
    row = lax.broadcasted_iota(jnp.int32, (HEAD_DIM, HEAD_DIM), 0)
    col = lax.broadcasted_iota(jnp.int32, (HEAD_DIM, HEAD_DIM), 1)
    eye = row == col
    ones = jnp.ones((HEAD_DIM, HEAD_DIM), BF16)

    def lane_broadcast_columns(rows):
        diag = jnp.concatenate(
            [jnp.where(eye, jnp.broadcast_to(rows[i:i + 1, :], (HEAD_DIM, HEAD_DIM)), 0.0)
             for i in range(rows.shape[0])], axis=0)
        return jnp.dot(diag.astype(BF16), ones, preferred_element_type=F32)

    for h in range(RET_HEADS):
        cols = slice(h * HEAD_DIM, (h + 1) * HEAD_DIM)
        q = _rope(z_ref[:, OFF_Q + h * HEAD_DIM:OFF_Q + (h + 1) * HEAD_DIM], cosf, sinf)
        k = _rope(z_ref[:, OFF_K + h * HEAD_DIM:OFF_K + (h + 1) * HEAD_DIM], cosf, sinf) * K_SCALE
        v = z_ref[:, OFF_V + h * HEAD_DIM:OFF_V + (h + 1) * HEAD_DIM]
        g = z_ref[:, OFF_G + h * HEAD_DIM:OFF_G + (h + 1) * HEAD_DIM]
        qdec = qdec_ref[h]
        kdec = kdec_ref[h]
        cdec = cdec_ref[h]
        sc = jnp.sum(q * k, axis=-1, keepdims=True) * dmask_ref[h]
        qk_cols = lane_broadcast_columns(jnp.concatenate([q * qdec, k * kdec], axis=0))
        o_rows = []
        for b in range(tb):
            s_old = s_ref[b, h]
            q_col = qk_cols[b * HEAD_DIM:(b + 1) * HEAD_DIM, :]
            k_col = qk_cols[(tb + b) * HEAD_DIM:(tb + b + 1) * HEAD_DIM, :]
            v_row = v[b:b + 1, :]
            o_rows.append(jnp.sum(q_col * s_old, axis=0, keepdims=True))
            so_ref[b, h] = s_old * cdec + k_col * v_row
        o = sc * v + jnp.concatenate(o_rows, axis=0)
        o = o * lax.rsqrt(jnp.mean(o * o, axis=-1, keepdims=True) + EPS)
        mix_ref[:, cols] = o * _silu(g)

    vn = _layernorm_rows(_gelu(z_ref[:, OFF_VG:OFF_VG + GM_WIDTH]), lng_ref[...], lnb_ref[...])
    vn_ref[...] = vn
    u = _gelu(z_ref[:, OFF_U:OFF_U + GM_WIDTH])
    mix_ref[:, RET_QK:RET_QK + GM_WIDTH] = u * (w00_ref[...] * vn + bs0_ref[...])


def _sample_mixer(z, state, cosf, sinf, dmask, qdec, kdec, cdec, lng, lnb, w00, bs0, *, tb):
    nb = z.shape[0]
    const3 = lambda i: (0, 0, 0)
    const2 = lambda i: (0, 0)
    state_spec = pl.BlockSpec((tb, RET_HEADS, HEAD_DIM, HEAD_DIM), lambda i: (i, 0, 0, 0))
    return pl.pallas_call(
        functools.partial(_sample_mixer_kernel, tb=tb),
        grid=(nb // tb,),
        in_specs=[
            pl.BlockSpec((tb, IN_COLS), lambda i: (i, 0)),
            state_spec,
            pl.BlockSpec((1, HEAD_DIM), const2),
            pl.BlockSpec((1, HEAD_DIM), const2),
            pl.BlockSpec((RET_HEADS, 1, 1), const3),
            pl.BlockSpec((RET_HEADS, 1, HEAD_DIM), const3),
            pl.BlockSpec((RET_HEADS, 1, HEAD_DIM), const3),
            pl.BlockSpec((RET_HEADS, 1, HEAD_DIM), const3),
            pl.BlockSpec((1, GM_WIDTH), const2),
            pl.BlockSpec((1, GM_WIDTH), const2),
            pl.BlockSpec((1, GM_WIDTH), const2),
            pl.BlockSpec((1, GM_WIDTH), const2),
        ],
        out_specs=[
            pl.BlockSpec((tb, D_MODEL), lambda i: (i, 0)),
            state_spec,
            pl.BlockSpec((tb, GM_WIDTH), lambda i: (i, 0)),
        ],
        out_shape=[
            jax.ShapeDtypeStruct((nb, D_MODEL), F32),
            jax.ShapeDtypeStruct(state.shape, F32),
            jax.ShapeDtypeStruct((nb, GM_WIDTH), F32),
        ],
        compiler_params=_params(("arbitrary",)),
        name="sample_mixer",
    )(z, state, cosf, sinf, dmask, qdec, kdec, cdec, lng, lnb, w00, bs0)


def _attn_out_kernel(mix_ref, w32_ref, h_ref, g_ref, mixs_ref, hs_ref,
                     h1_ref, xn_ref, h1s_ref, xns_ref, w_ref, *, row_block, row_chunk):
    @pl.when(pl.program_id(0) == 0)
    def _():
        _round_weight(w32_ref, w_ref)
        h1s = hs_ref[...] + jnp.dot(mixs_ref[...].astype(BF16), w_ref[...],
                                    preferred_element_type=F32)
        h1s_ref[...] = h1s
        xns_ref[...] = _rms_rows(h1s, g_ref[...]).astype(BF16)

    tm = h_ref.shape[0]
    for r_lo in range(0, tm, row_block):
        rows = slice(r_lo, r_lo + row_block)
        h1_ref[rows, :] = h_ref[rows, :] + jnp.dot(mix_ref[rows, :], w_ref[...],
                                                   preferred_element_type=F32)
        for r0 in range(r_lo, r_lo + row_block, row_chunk):
            sub = slice(r0, r0 + row_chunk)
            xn_ref[sub, :] = _rms_rows(h1_ref[sub, :], g_ref[...]).astype(BF16)


def _attn_out(mix, w, h, g, mix_s, h_s, *, tm):
    m, d = h.shape
    nb = h_s.shape[0]
    once = pl.Buffered(1)
    const = lambda i: (0, 0)
    tile = pl.BlockSpec((tm, d), lambda i: (i, 0))
    return pl.pallas_call(
        functools.partial(_attn_out_kernel, row_block=256, row_chunk=32),
        grid=(m // tm,),
        in_specs=[
            tile,
            pl.BlockSpec((d, d), const, pipeline_mode=once),
            tile,
            pl.BlockSpec((1, d), const, pipeline_mode=once),
            pl.BlockSpec((nb, d), const, pipeline_mode=once),
            pl.BlockSpec((nb, d), const, pipeline_mode=once),
        ],
        out_specs=[tile, tile, pl.BlockSpec((nb, d), const), pl.BlockSpec((nb, d), const)],
        out_shape=[
            jax.ShapeDtypeStruct((m, d), F32),
            jax.ShapeDtypeStruct((m, d), BF16),
            jax.ShapeDtypeStruct((nb, d), F32),
            jax.ShapeDtypeStruct((nb, d), BF16),
        ],
        scratch_shapes=[pltpu.VMEM((d, d), BF16)],
        compiler_params=_params(("arbitrary",)),
        name="attn_out",
    )(mix, w, h, g, mix_s, h_s)


CONV_PAD = 8


def _ffn_up_kernel(xn_ref, wg32_ref, wu32_ref, cg_ref, cu_ref, xs_ref, s0g_ref, s0u_ref, s1g_ref,
                   s1u_ref, act_ref, csg_ref, csu_ref, asg_ref, asu_ref, acts_ref,
                   ag_ref, sg_ref, wg_ref, wu_ref, *au_refs, tiles_per_seq, row_chunk, up_splits):
    i = pl.program_id(1)
    tm, tn = act_ref.shape
    first = (i % tiles_per_seq) == 0
    last_rows = up_splits[-1]

    @pl.when(first)
    def _():
        ag_ref[0:CONV_PAD, :] = jnp.zeros((CONV_PAD, tn), F32)
        au_refs[0][0:CONV_PAD, :] = jnp.zeros((CONV_PAD, tn), F32)

    @pl.when(jnp.logical_not(first))
    def _():
        ag_ref[0:CONV_PAD, :] = ag_ref[tm:tm + CONV_PAD, :]
        au_refs[0][0:CONV_PAD, :] = au_refs[-1][last_rows:last_rows + CONV_PAD, :]

    @pl.when(i == 0)
    def _():
        _round_weight(wg32_ref, wg_ref)
        _round_weight(wu32_ref, wu_ref)
        xs = xs_ref[...]
        nb = xs.shape[0]
        a_g = jnp.dot(xs, wg_ref[...], preferred_element_type=F32)
        a_u = jnp.dot(xs, wu_ref[...], preferred_element_type=F32)
        asg_ref[...] = a_g
        asu_ref[...] = a_u

        def conv_s(s0_ref, s1_ref, a, c_ref):
            t = lambda v: v.reshape(nb // 8, 8, tn)
            return (t(s0_ref[...]) * c_ref[0] + t(s1_ref[...]) * c_ref[1] + t(a) * c_ref[2]
                    + c_ref[3]).reshape(nb, tn)
        acts_ref[...] = (_silu(conv_s(s0g_ref, s1g_ref, a_g, cg_ref))
                         * conv_s(s0u_ref, s1u_ref, a_u, cu_ref)).astype(acts_ref.dtype)

    def conv(a_ref, c_ref, r0):
        def window(back):
            lo = r0 + CONV_PAD - back
            return a_ref[lo:lo + row_chunk, :].reshape(row_chunk // 8, 8, tn)
        return window(2) * c_ref[0] + window(1) * c_ref[1] + window(0) * c_ref[2] + c_ref[3]

    ag_ref[CONV_PAD:CONV_PAD + tm, :] = jnp.dot(xn_ref[...], wg_ref[...],
                                                preferred_element_type=F32)
    for r0 in range(0, tm, row_chunk):
        gate = conv(ag_ref, cg_ref, r0).reshape(row_chunk, tn).astype(BF16)
        sg_ref[r0:r0 + row_chunk, :] = _silu(gate)
    start = 0
    for idx, rows in enumerate(up_splits):
        au_ref = au_refs[idx]
        au_ref[CONV_PAD:CONV_PAD + rows, :] = jnp.dot(xn_ref[start:start + rows, :], wu_ref[...],
                                                      preferred_element_type=F32)
        if idx + 1 < len(up_splits):
            au_refs[idx + 1][0:CONV_PAD, :] = au_ref[rows:rows + CONV_PAD, :]
        start += rows
    start = 0
    for idx, rows in enumerate(up_splits):
        for r0 in range(0, rows, row_chunk):
            up = conv(au_refs[idx], cu_ref, r0).reshape(row_chunk, tn)
            out = slice(start + r0, start + r0 + row_chunk)
            act_ref[out, :] = sg_ref[out, :] * up.astype(BF16)
        start += rows

    @pl.when((i % tiles_per_seq) == tiles_per_seq - 1)
    def _():
        csg_ref[0] = ag_ref[CONV_PAD + tm - 2:CONV_PAD + tm, :]
        csu_ref[0] = au_refs[-1][CONV_PAD + last_rows - 2:CONV_PAD + last_rows, :]


def _ffn_up(xn, w_up, conv_taps, xn_s, conv_state, *, batch, seq, tm, tn):
    m, d = xn.shape
    nb = xn_s.shape[0]
    n_col = D_FF // tn
    tiles_per_seq = seq // tm
    up_splits = (tm // 2, tm // 2)
    sample_cols = lambda k: pl.BlockSpec((nb, tn), lambda j, i: (0, k * n_col + j))
    sample_out = pl.BlockSpec((nb, tn), lambda j, i: (0, j))
    return pl.pallas_call(
        functools.partial(_ffn_up_kernel, tiles_per_seq=tiles_per_seq, row_chunk=64,
                          up_splits=up_splits),
        grid=(n_col, m // tm),
        in_specs=[
            pl.BlockSpec((tm, d), lambda j, i: (i, 0)),
            pl.BlockSpec((d, tn), lambda j, i: (0, j)),
            pl.BlockSpec((d, tn), lambda j, i: (0, n_col + j)),
            pl.BlockSpec((4, 8, tn), lambda j, i: (0, 0, j)),
            pl.BlockSpec((4, 8, tn), lambda j, i: (0, 0, n_col + j)),
            pl.BlockSpec((nb, d), lambda j, i: (0, 0)),
            sample_cols(0), sample_cols(1), sample_cols(2), sample_cols(3),
        ],
        out_specs=[
            pl.BlockSpec((tm, tn), lambda j, i: (i, j)),
            pl.BlockSpec((1, 2, tn), lambda j, i: (i // tiles_per_seq, 0, j)),
            pl.BlockSpec((1, 2, tn), lambda j, i: (i // tiles_per_seq, 0, j)),
            sample_out, sample_out, sample_out,
        ],
        out_shape=[
            jax.ShapeDtypeStruct((m, D_FF), BF16),
            jax.ShapeDtypeStruct((batch, 2, D_FF), F32),
            jax.ShapeDtypeStruct((batch, 2, D_FF), F32),
            jax.ShapeDtypeStruct((nb, D_FF), F32),
            jax.ShapeDtypeStruct((nb, D_FF), F32),
            jax.ShapeDtypeStruct((nb, D_FF), BF16),
        ],
        scratch_shapes=[pltpu.VMEM((tm + CONV_PAD, tn), F32),
                        pltpu.VMEM((tm, tn), BF16),
                        pltpu.VMEM((d, tn), BF16),
                        pltpu.VMEM((d, tn), BF16)]
                       + [pltpu.VMEM((rows + CONV_PAD, tn), F32) for rows in up_splits],
        compiler_params=_params(("arbitrary", "arbitrary")),
        name="ffn_up",
    )(xn, w_up, w_up, conv_taps, conv_taps, xn_s, conv_state, conv_state, conv_state, conv_state)


def _ffn_down_kernel(act_ref, w32_ref, h_ref, acts_ref, hs_ref, o_ref, os_ref, w_ref):
    @pl.when(pl.program_id(1) == 0)
    def _():
        _round_weight(w32_ref, w_ref)
        os_ref[...] = hs_ref[...] + jnp.dot(acts_ref[...], w_ref[...], preferred_element_type=F32)

    o_ref[...] = h_ref[...] + jnp.dot(act_ref[...], w_ref[...], preferred_element_type=F32)


def _ffn_down(act, w_down, h1, act_s, h1_s, *, tm, tn):
    m, k = act.shape
    n = w_down.shape[1]
    nb = act_s.shape[0]
    return pl.pallas_call(
        _ffn_down_kernel,
        grid=(n // tn, m // tm),
        in_specs=[
            pl.BlockSpec((tm, k), lambda j, i: (i, 0)),
            pl.BlockSpec((k, tn), lambda j, i: (0, j), pipeline_mode=pl.Buffered(1)),
            pl.BlockSpec((tm, tn), lambda j, i: (i, j)),
            pl.BlockSpec((nb, k), lambda j, i: (0, 0)),
            pl.BlockSpec((nb, tn), lambda j, i: (0, j)),
        ],
        out_specs=[
            pl.BlockSpec((tm, tn), lambda j, i: (i, j)),
            pl.BlockSpec((nb, tn), lambda j, i: (0, j)),
        ],
        out_shape=[
            jax.ShapeDtypeStruct((m, n), F32),
            jax.ShapeDtypeStruct((nb, n), F32),
        ],
        scratch_shapes=[pltpu.VMEM((k, tn), BF16)],
        compiler_params=pltpu.CompilerParams(dimension_semantics=("arbitrary", "arbitrary"),
                                             vmem_limit_bytes=FFN_DOWN_VMEM_LIMIT),
        name="ffn_down",
    )(act, w_down, h1, act_s, h1_s)


def _ple_rows(h_ref, p_ref, gple_ref, wg_ref, wp_ref, gfin_ref, y_ref, xn_ref, r_lo, n_rows,
              row_chunk):
    rows = slice(r_lo, r_lo + n_rows)
    for r0 in range(r_lo, r_lo + n_rows, row_chunk):
        sub = slice(r0, r0 + row_chunk)
        xn_ref[sub, :] = _rms_rows(h_ref[sub, :], gple_ref[...]).astype(BF16)
    gate = jax.nn.sigmoid(jnp.dot(xn_ref[rows, :], wg_ref[...], preferred_element_type=F32))
    proj = jnp.dot(p_ref[rows, :].astype(BF16), wp_ref[...], preferred_element_type=F32)
    y_ref[rows, :] = h_ref[rows, :] + proj * gate
    for r0 in range(r_lo, r_lo + n_rows, row_chunk):
        sub = slice(r0, r0 + row_chunk)
        y_ref[sub, :] = _rms_rows(y_ref[sub, :], gfin_ref[...])


def _ple_final_kernel(h_ref, p_ref, gple_ref, wg32_ref, wp32_ref, gfin_ref, hs_ref, ps_ref,
                      y_ref, ys_ref, xn_ref, wg_ref, wp_ref, *, row_block, row_chunk):
    @pl.when(pl.program_id(0) == 0)
    def _():
        _round_weight(wg32_ref, wg_ref)
        _round_weight(wp32_ref, wp_ref)
        _ple_rows(hs_ref, ps_ref, gple_ref, wg_ref, wp_ref, gfin_ref, ys_ref, xn_ref, 0,
                  hs_ref.shape[0], row_chunk)

    for r_lo in range(0, h_ref.shape[0], row_block):
        _ple_rows(h_ref, p_ref, gple_ref, wg_ref, wp_ref, gfin_ref, y_ref, xn_ref, r_lo, row_block,
                  row_chunk)


def _ple_final(h, p, g_ple, w_gate, w_proj, g_final, h_s, p_s, *, tm):
    m, d = h.shape
    pd = p.shape[1]
    nb = h_s.shape[0]
    assert nb <= tm
    once = pl.Buffered(1)
    const = lambda i: (0, 0)
    return pl.pallas_call(
        functools.partial(_ple_final_kernel, row_block=256, row_chunk=32),
        grid=(m // tm,),
        in_specs=[
            pl.BlockSpec((tm, d), lambda i: (i, 0)),
            pl.BlockSpec((tm, pd), lambda i: (i, 0)),
            pl.BlockSpec((1, d), const, pipeline_mode=once),
            pl.BlockSpec((d, d), const, pipeline_mode=once),
            pl.BlockSpec((pd, d), const, pipeline_mode=once),
            pl.BlockSpec((1, d), const, pipeline_mode=once),
            pl.BlockSpec((nb, d), const, pipeline_mode=once),
            pl.BlockSpec((nb, pd), const, pipeline_mode=once),
        ],
        out_specs=[pl.BlockSpec((tm, d), lambda i: (i, 0)), pl.BlockSpec((nb, d), const)],
        out_shape=[jax.ShapeDtypeStruct((m, d), F32), jax.ShapeDtypeStruct((nb, d), F32)],
        scratch_shapes=[pltpu.VMEM((tm, d), BF16), pltpu.VMEM((d, d), BF16),
                        pltpu.VMEM((pd, d), BF16)],
        compiler_params=_params(("arbitrary",)),
        name="ple_final",
    )(h, p, g_ple, w_gate, w_proj, g_final, h_s, p_s)


def _rope_tables(pos):
    inv = ROPE_THETA ** (-np.arange(0, HEAD_DIM, 2, dtype=np.float64) / HEAD_DIM)
    ang = np.asarray(pos, np.float64)[:, None] * inv[None, :]
    cos, sin = np.cos(ang), np.sin(ang)
    return (jnp.asarray(np.concatenate([cos, cos], axis=-1), F32),
            jnp.asarray(np.concatenate([-sin, sin], axis=-1), F32))


def _decay_tables(c):
    log_g = np.log1p(-np.exp2(-5.0 - np.arange(RET_HEADS, dtype=np.float64)))
    idx = np.arange(c, dtype=np.float64)
    diff = idx[:, None] - idx[None, :]
    dmask = np.where(diff >= 0, np.exp(log_g[:, None, None] * np.maximum(diff, 0.0)), 0.0)
    q_dec = np.exp(log_g[:, None] * (idx + 1.0))[..., None]
    k_dec = np.exp(log_g[:, None] * (c - 1.0 - idx))[..., None]
    c_dec = np.exp(log_g * c)[:, None, None]
    return tuple(jnp.asarray(t, F32) for t in (dmask, q_dec, k_dec, c_dec))


def kernel(x_prompt, x_sample, p_prompt, p_sample, state_ret, state_conv, g_attn, w_in, gm_ln_g,
           gm_ln_b, gm_ws, gm_bs, w_o, g_ffn, w_up, conv_w, conv_b, w_down, g_ple, w_ple_gate,
           w_ple_proj, g_final):
    batch, seq, d = x_prompt.shape
    nb = x_sample.shape[0]
    assert x_sample.shape[1] == 1 and g_attn.shape[0] == 1

    w_in_b = w_in[0].astype(BF16)

    g_attn2, g_ffn2, g_ple2 = g_attn[0][None], g_ffn[0][None], g_ple[0][None]
    g_fin2 = g_final[None]
    lng, lnb = gm_ln_g[0][None], gm_ln_b[0][None]
    ws, bs = gm_ws[0], gm_bs[0]
    cw, cb = conv_w[0], conv_b[0][None]
    n_groups = GM_WIDTH // HEAD_DIM

    cos_p, sin_p = _rope_tables(np.arange(seq))
    dmask, q_dec, k_dec, c_dec = _decay_tables(CHUNK)
    qdec_p = jnp.broadcast_to(q_dec, (RET_HEADS, CHUNK, HEAD_DIM))
    kdec_p = jnp.broadcast_to(k_dec, (RET_HEADS, CHUNK, HEAD_DIM))
    cdec_p = jnp.broadcast_to(c_dec, (RET_HEADS, 1, HEAD_DIM))
    bsf = jnp.broadcast_to(bs.T[:, :, None], (CHUNK, n_groups, HEAD_DIM)).reshape(CHUNK, GM_WIDTH)
    cos_s, sin_s = _rope_tables(PAST_LEN + np.arange(1))
    dmask1, q_dec1, k_dec1, c_dec1 = _decay_tables(1)
    qdec_s = jnp.broadcast_to(q_dec1, (RET_HEADS, 1, HEAD_DIM))
    kdec_s = jnp.broadcast_to(k_dec1, (RET_HEADS, 1, HEAD_DIM))
    cdec_s = jnp.broadcast_to(c_dec1, (RET_HEADS, 1, HEAD_DIM))
    w00 = jnp.broadcast_to(ws[:, 0, 0][:, None], (n_groups, HEAD_DIM)).reshape(1, GM_WIDTH)
    bs0 = jnp.broadcast_to(bs[:, 0][:, None], (n_groups, HEAD_DIM)).reshape(1, GM_WIDTH)
    conv_taps = jnp.broadcast_to(jnp.concatenate([cw, cb], axis=0)[:, None, :], (4, 8, 2 * D_FF))

    xp = x_prompt.reshape(batch * seq, d)
    xs = x_sample.reshape(nb, d)

    mix_p, ret_p = _in_mixer(xp, g_attn2, w_in_b, batch, seq, cos_p, sin_p, dmask, qdec_p, kdec_p,
                             cdec_p, lng, lnb, ws, bsf, tm=512)
    z_s = _norm_matmul(xs, g_attn2, w_in_b, tm=nb, tn=1024, name="sample_in_proj")
    mix_s, ret_s, vn_s = _sample_mixer(z_s, state_ret[0], cos_s, sin_s, dmask1, qdec_s, kdec_s,
                                       cdec_s, lng, lnb, w00, bs0, tb=8)
    h1_p, xn2_p, h1_s, xn2_s = _attn_out(mix_p, w_o[0], xp, g_ffn2, mix_s, xs, tm=512)

    act_p, csg_p, csu_p, asg_s, asu_s, act_s = _ffn_up(
        xn2_p, w_up[0], conv_taps, xn2_s, state_conv[0].reshape(nb, 2 * 2 * D_FF), batch=batch,
        seq=seq, tm=1024, tn=512)
    h2_p, h2_s = _ffn_down(act_p, w_down[0], h1_p, act_s, h1_s, tm=512, tn=1024)

    y_p, y_s = _ple_final(h2_p, p_prompt[0].reshape(batch * seq, PLE_DIM), g_ple2, w_ple_gate[0],
                          w_ple_proj[0], g_fin2, h2_s, p_sample[0].reshape(nb, PLE_DIM), tm=512)

    conv_p = jnp.concatenate([csg_p, csu_p], axis=-1)[None]
    a_s = jnp.concatenate([asg_s, asu_s], axis=-1)
    conv_s = jnp.stack([state_conv[0][:, 1, :], a_s], axis=1)[None]
    return (y_p.reshape(batch, seq, d), y_s.reshape(nb, 1, d), ret_p[None], conv_p,
            ret_s[None], conv_s, vn_s.reshape(1, nb, 1, GM_WIDTH))
```

```python
import functools
import math

import jax
import jax.numpy as jnp
import numpy as np
from jax import lax
from jax.experimental import pallas as pl
from jax.experimental.pallas import tpu as pltpu

F32 = jnp.float32
BF16 = jnp.bfloat16

D_MODEL = 2048
RET_HEADS = 8
HEAD_DIM = 128
CHUNK = 128
RET_QK = RET_HEADS * HEAD_DIM
GM_WIDTH = 1024
IN_COLS = 6144
D_FF = 5632
PLE_DIM = 256
ROPE_THETA = 10000.0
PAST_LEN = 16384
EPS = 1e-6
K_SCALE = HEAD_DIM ** -0.5

OFF_Q, OFF_K, OFF_V, OFF_G, OFF_U, OFF_VG = 0, 1024, 2048, 3072, 4096, 5120

VMEM_LIMIT = 56 * 1024 * 1024
LANES = 128
IN_MIXER_VMEM_LIMIT = 62 * 1024 * 1024
FFN_DOWN_VMEM_LIMIT = 60 * 1024 * 1024
WEIGHT_CAST_ROWS = 256


def _params(semantics):
    return pltpu.CompilerParams(dimension_semantics=semantics, vmem_limit_bytes=VMEM_LIMIT)


def _rms_rows(x, g):
    ms = jnp.mean(x * x, axis=-1, keepdims=True)
    return x * lax.rsqrt(ms + EPS) * g


def _gelu(x):
    return jax.nn.gelu(x)


def _silu(x):
    return x * jax.nn.sigmoid(x)


def _round_weight(w32_ref, w_ref):
    for k0 in range(0, w_ref.shape[0], WEIGHT_CAST_ROWS):
        rows = slice(k0, min(k0 + WEIGHT_CAST_ROWS, w_ref.shape[0]))
        w_ref[rows, :] = w32_ref[rows, :].astype(BF16)


def _norm_matmul_kernel(x_ref, g_ref, w_ref, o_ref, xn_ref, *, row_chunk):
    @pl.when(pl.program_id(1) == 0)
    def _():
        def body(r, carry):
            rows = pl.ds(pl.multiple_of(r * row_chunk, row_chunk), row_chunk)
            xn_ref[rows, :] = _rms_rows(x_ref[rows, :], g_ref[...]).astype(BF16)
            return carry
        lax.fori_loop(0, x_ref.shape[0] // row_chunk, body, 0)

    o_ref[...] = jnp.dot(xn_ref[...], w_ref[...], preferred_element_type=F32)


def _norm_matmul(x, g, w, *, tm, tn, name):
    m, k = x.shape
    n = w.shape[1]
    row_chunk = min(tm, 32)
    return pl.pallas_call(
        functools.partial(_norm_matmul_kernel, row_chunk=row_chunk),
        grid=(m // tm, n // tn),
        in_specs=[
            pl.BlockSpec((tm, k), lambda i, j: (i, 0)),
            pl.BlockSpec((1, k), lambda i, j: (0, 0)),
            pl.BlockSpec((k, tn), lambda i, j: (0, j)),
        ],
        out_specs=pl.BlockSpec((tm, tn), lambda i, j: (i, j)),
        out_shape=jax.ShapeDtypeStruct((m, n), F32),
        scratch_shapes=[pltpu.VMEM((tm, k), BF16)],
        compiler_params=_params(("arbitrary", "arbitrary")),
        name=name,
    )(x, g, w)


def _rope(x, cosf, sinf):
    return x * cosf + pltpu.roll(x, HEAD_DIM // 2, 1) * sinf


def _layernorm_rows(x, g, b):
    mu = jnp.mean(x, axis=-1, keepdims=True)
    xc = x - mu
    var = jnp.mean(xc * xc, axis=-1, keepdims=True)
    return xc * lax.rsqrt(var + EPS) * g + b


def _retention_heads(z_ref, r0, heads, cosf, sinf, dmask_ref, qdec_ref, kdec_ref, cdec_ref,
                     mix_ref, out_r0, state_refs):
    zr = slice(r0, r0 + CHUNK)
    orows = slice(out_r0, out_r0 + CHUNK)
    staged = []
    for h in heads:
        q = _rope(z_ref[zr, OFF_Q + h * HEAD_DIM:OFF_Q + (h + 1) * HEAD_DIM], cosf, sinf)
        k = _rope(z_ref[zr, OFF_K + h * HEAD_DIM:OFF_K + (h + 1) * HEAD_DIM], cosf, sinf) * K_SCALE
        v = z_ref[zr, OFF_V + h * HEAD_DIM:OFF_V + (h + 1) * HEAD_DIM].astype(BF16)
        s_old = state_refs[h][...]
        sc = lax.dot_general(q.astype(BF16), k.astype(BF16), (((1,), (1,)), ((), ())),
                             preferred_element_type=F32) * dmask_ref[h]
        o_cross = jnp.dot((q * qdec_ref[h]).astype(BF16), s_old.astype(BF16),
                          preferred_element_type=F32)
        state_refs[h][...] = s_old * cdec_ref[h] + lax.dot_general(
            (k * kdec_ref[h]).astype(BF16), v, (((0,), (0,)), ((), ())),
            preferred_element_type=F32)
        staged.append((h, sc, o_cross, v))
    for h, sc, o_cross, v in staged:
        o = jnp.dot(sc.astype(BF16), v, preferred_element_type=F32) + o_cross
        o = o * lax.rsqrt(jnp.mean(o * o, axis=-1, keepdims=True) + EPS)
        g = z_ref[zr, OFF_G + h * HEAD_DIM:OFF_G + (h + 1) * HEAD_DIM]
        mix_ref[orows, h * HEAD_DIM:(h + 1) * HEAD_DIM] = (o * _silu(g)).astype(mix_ref.dtype)


def _gating_norm(z_ref, r0, lng_ref, lnb_ref, vn_ref):
    zr = slice(r0, r0 + CHUNK)
    vn = _layernorm_rows(_gelu(z_ref[zr, OFF_VG:OFF_VG + GM_WIDTH]), lng_ref[...], lnb_ref[...])
    vn_ref[...] = vn.astype(BF16)


def _gating_groups(z_ref, r0, groups, vn_ref, wm_ref, bsf_ref, mix_ref, out_r0):
    zr = slice(r0, r0 + CHUNK)
    orows = slice(out_r0, out_r0 + CHUNK)
    for grp in groups:
        cols = slice(grp * HEAD_DIM, (grp + 1) * HEAD_DIM)
        sp = jnp.dot(wm_ref[grp], vn_ref[:, cols], preferred_element_type=F32) + bsf_ref[:, cols]
        u = _gelu(z_ref[zr, OFF_U + grp * HEAD_DIM:OFF_U + (grp + 1) * HEAD_DIM])
        mix_ref[orows, RET_QK + grp * HEAD_DIM:RET_QK + (grp + 1) * HEAD_DIM] = (
            (u * sp).astype(mix_ref.dtype))


N_SLOTS = 8


def _in_mixer_kernel(x_ref, g_ref, w_ref, cos_ref, sin_ref, dmask_ref, qdec_ref, kdec_ref, cdec_ref,
                     lng_ref, lnb_ref, ws_ref, bsf_ref, mix_ref, s_ref,
                     xn_a, xn_b, z_a, z_b, wm_ref, vn_0, vn_1, vn_2, vn_3, *state_refs,
                     steps_per_seq, row_chunk):
    step = pl.program_id(0)
    n_groups = GM_WIDTH // HEAD_DIM

    @pl.when(step % steps_per_seq == 0)
    def _():
        for state_ref in state_refs:
            state_ref[...] = jnp.zeros_like(state_ref)

    @pl.when(step == 0)
    def _():
        row = lax.broadcasted_iota(jnp.int32, (CHUNK, CHUNK), 0)
        col = lax.broadcasted_iota(jnp.int32, (CHUNK, CHUNK), 1)
        for grp in range(n_groups):
            wm_ref[grp] = jnp.where(row >= col, ws_ref[grp], 0.0).astype(BF16)

    half = x_ref.shape[0] // 2
    assert half == 2 * CHUNK and RET_HEADS == N_SLOTS and n_groups == N_SLOTS

    def norm_rows(xn_ref, base):
        for r0 in range(0, half, row_chunk):
            xn_ref[r0:r0 + row_chunk, :] = _rms_rows(
                x_ref[base + r0:base + r0 + row_chunk, :], g_ref[...]).astype(BF16)

    def heads(z_ref, c0, base, hs):
        pos = slice(base + c0, base + c0 + CHUNK)
        _retention_heads(z_ref, c0, hs, cos_ref[pos, :], sin_ref[pos, :], dmask_ref, qdec_ref,
                         kdec_ref, cdec_ref, mix_ref, base + c0, state_refs)

    def groups(z_ref, c0, base, vn_ref, gs):
        _gating_groups(z_ref, c0, gs, vn_ref, wm_ref, bsf_ref, mix_ref, base + c0)

    norm_rows(xn_a, 0)
    z_a[...] = jnp.dot(xn_a[...], w_ref[...], preferred_element_type=F32)
    norm_rows(xn_b, half)

    piece = w_ref.shape[1] // N_SLOTS
    first_gating_piece = OFF_U // piece
    order = list(range(first_gating_piece, N_SLOTS)) + list(range(first_gating_piece))
    gating_ready = N_SLOTS - first_gating_piece
    assert gating_ready + 4 <= N_SLOTS
    xnb = xn_b[...]
    for slot in range(N_SLOTS):
        cols = slice(order[slot] * piece, (order[slot] + 1) * piece)
        z_b[:, cols] = jnp.dot(xnb, w_ref[:, cols], preferred_element_type=F32)
        c0 = (slot // 4) * CHUNK
        if slot == 0:
            _gating_norm(z_a, 0, lng_ref, lnb_ref, vn_0)
        if slot == 3:
            _gating_norm(z_a, CHUNK, lng_ref, lnb_ref, vn_1)
        heads(z_a, c0, 0, (2 * (slot % 4), 2 * (slot % 4) + 1))
        if slot in (1, 2):
            groups(z_a, 0, 0, vn_0, range(4 * (slot - 1), 4 * slot))
        if slot in (4, 5):
            groups(z_a, CHUNK, 0, vn_1, range(4 * (slot - 4), 4 * (slot - 3)))
        late = slot - gating_ready
        if 0 <= late < 4:
            b_c0, b_vn = ((0, vn_2), (CHUNK, vn_3))[late // 2]
            if late % 2 == 0:
                _gating_norm(z_b, b_c0, lng_ref, lnb_ref, b_vn)
            groups(z_b, b_c0, half, b_vn, range(4 * (late % 2), 4 * (late % 2) + 4))

    for c0 in (0, CHUNK):
        heads(z_b, c0, half, range(RET_HEADS))
    for h, state_ref in enumerate(state_refs):
        s_ref[0, h] = state_ref[...]


def _in_mixer(x, g, w, batch, seq, cosf, sinf, dmask, qdec, kdec, cdec, lng, lnb, ws, bsf, *, tm):
    m, d = x.shape
    n = w.shape[1]
    steps_per_seq = seq // tm
    n_groups = GM_WIDTH // HEAD_DIM
    once = pl.Buffered(1)
    const3 = lambda s: (0, 0, 0)
    const2 = lambda s: (0, 0)
    return pl.pallas_call(
        functools.partial(_in_mixer_kernel, steps_per_seq=steps_per_seq, row_chunk=32),
        grid=(m // tm,),
        in_specs=[
            pl.BlockSpec((tm, d), lambda s: (s, 0)),
            pl.BlockSpec((1, d), const2, pipeline_mode=once),
            pl.BlockSpec((d, n), const2, pipeline_mode=once),
            pl.BlockSpec((tm, HEAD_DIM), lambda s: (s % steps_per_seq, 0)),
            pl.BlockSpec((tm, HEAD_DIM), lambda s: (s % steps_per_seq, 0)),
            pl.BlockSpec((RET_HEADS, CHUNK, CHUNK), const3, pipeline_mode=once),
            pl.BlockSpec((RET_HEADS, CHUNK, HEAD_DIM), const3, pipeline_mode=once),
            pl.BlockSpec((RET_HEADS, CHUNK, HEAD_DIM), const3, pipeline_mode=once),
            pl.BlockSpec((RET_HEADS, 1, HEAD_DIM), const3, pipeline_mode=once),
            pl.BlockSpec((1, GM_WIDTH), const2, pipeline_mode=once),
            pl.BlockSpec((1, GM_WIDTH), const2, pipeline_mode=once),
            pl.BlockSpec((n_groups, CHUNK, CHUNK), const3, pipeline_mode=once),
            pl.BlockSpec((CHUNK, GM_WIDTH), const2, pipeline_mode=once),
        ],
        out_specs=[
            pl.BlockSpec((tm, D_MODEL), lambda s: (s, 0)),
            pl.BlockSpec((1, RET_HEADS, HEAD_DIM, HEAD_DIM), lambda s: (s // steps_per_seq, 0, 0, 0)),
        ],
        out_shape=[
            jax.ShapeDtypeStruct((m, D_MODEL), BF16),
            jax.ShapeDtypeStruct((batch, RET_HEADS, HEAD_DIM, HEAD_DIM), F32),
        ],
        scratch_shapes=[
            pltpu.VMEM((tm // 2, d), BF16), pltpu.VMEM((tm // 2, d), BF16),
            pltpu.VMEM((tm // 2, n), F32), pltpu.VMEM((tm // 2, n), F32),
            pltpu.VMEM((n_groups, CHUNK, CHUNK), BF16),
        ] + [pltpu.VMEM((CHUNK, GM_WIDTH), BF16) for _ in range(4)]
          + [pltpu.VMEM((HEAD_DIM, HEAD_DIM), F32) for _ in range(RET_HEADS)],
        compiler_params=pltpu.CompilerParams(dimension_semantics=("arbitrary",),
                                             vmem_limit_bytes=IN_MIXER_VMEM_LIMIT),
        name="prompt_in_mixer",
    )(x, g, w, cosf, sinf, dmask, qdec, kdec, cdec, lng, lnb, ws, bsf)


def _sample_mixer_kernel(z_ref, s_ref, cos_ref, sin_ref, dmask_ref, qdec_ref, kdec_ref, cdec_ref,
                         lng_ref, lnb_ref, w00_ref, bs0_ref, mix_ref, so_ref, vn_ref, *, tb):
    cosf = cos_ref[...]
    sinf = sin_ref[...]
    row = lax.broadcasted_iota(jnp.int32, (HEAD_DIM, HEAD_DIM), 0)
    col = lax.broadcasted_iota(jnp.int32, (HEAD_DIM, HEAD_DIM), 1)
    eye = row == col
    ones = jnp.ones((HEAD_DIM, HEAD_DIM), BF16)

    def lane_broadcast_columns(rows):
        diag = jnp.concatenate(
            [jnp.where(eye, jnp.broadcast_to(rows[i:i + 1, :], (HEAD_DIM, HEAD_DIM)), 0.0)
             for i in range(rows.shape[0])], axis=0)
        return jnp.dot(diag.astype(BF16), ones, preferred_element_type=F32)

    for h in range(RET_HEADS):
        cols = slice(h * HEAD_DIM, (h + 1) * HEAD_DIM)
        q = _rope(z_ref[:, OFF_Q + h * HEAD_DIM:OFF_Q + (h + 1) * HEAD_DIM], cosf, sinf)
        k = _rope(z_ref[:, OFF_K + h * HEAD_DIM:OFF_K + (h + 1) * HEAD_DIM], cosf, sinf) * K_SCALE
        v = z_ref[:, OFF_V + h * HEAD_DIM:OFF_V + (h + 1) * HEAD_DIM]
        g = z_ref[:, OFF_G + h * HEAD_DIM:OFF_G + (h + 1) * HEAD_DIM]
        qdec = qdec_ref[h]
        kdec = kdec_ref[h]
        cdec = cdec_ref[h]
        sc = jnp.sum(q * k, axis=-1, keepdims=True) * dmask_ref[h]
        qk_cols = lane_broadcast_columns(jnp.concatenate([q * qdec, k * kdec], axis=0))
        o_rows = []
        for b in range(tb):
            s_old = s_ref[b, h]
            q_col = qk_cols[b * HEAD_DIM:(b + 1) * HEAD_DIM, :]
            k_col = qk_cols[(tb + b) * HEAD_DIM:(tb + b + 1) * HEAD_DIM, :]
            v_row = v[b:b + 1, :]
            o_rows.append(jnp.sum(q_col * s_old, axis=0, keepdims=True))
            so_ref[b, h] = s_old * cdec + k_col * v_row
        o = sc * v + jnp.concatenate(o_rows, axis=0)
        o = o * lax.rsqrt(jnp.mean(o * o, axis=-1, keepdims=True) + EPS)
        mix_ref[:, cols] = o * _silu(g)

    vn = _layernorm_rows(_gelu(z_ref[:, OFF_VG:OFF_VG + GM_WIDTH]), lng_ref[...], lnb_ref[...])
    vn_ref[...] = vn
    u = _gelu(z_ref[:, OFF_U:OFF_U + GM_WIDTH])
    mix_ref[:, RET_QK:RET_QK + GM_WIDTH] = u * (w00_ref[...] * vn + bs0_ref[...])


def _sample_mixer(z, state, cosf, sinf, dmask, qdec, kdec, cdec, lng, lnb, w00, bs0, *, tb):
    nb = z.shape[0]
    const3 = lambda i: (0, 0, 0)
    const2 = lambda i: (0, 0)
    state_spec = pl.BlockSpec((tb, RET_HEADS, HEAD_DIM, HEAD_DIM), lambda i: (i, 0, 0, 0))
    return pl.pallas_call(
        functools.partial(_sample_mixer_kernel, tb=tb),
        grid=(nb // tb,),
        in_specs=[
            pl.BlockSpec((tb, IN_COLS), lambda i: (i, 0)),
            state_spec,
            pl.BlockSpec((1, HEAD_DIM), const2),
            pl.BlockSpec((1, HEAD_DIM), const2),
            pl.BlockSpec((RET_HEADS, 1, 1), const3),
            pl.BlockSpec((RET_HEADS, 1, HEAD_DIM), const3),
            pl.BlockSpec((RET_HEADS, 1, HEAD_DIM), const3),
            pl.BlockSpec((RET_HEADS, 1, HEAD_DIM), const3),
            pl.BlockSpec((1, GM_WIDTH), const2),
            pl.BlockSpec((1, GM_WIDTH), const2),
            pl.BlockSpec((1, GM_WIDTH), const2),
            pl.BlockSpec((1, GM_WIDTH), const2),
        ],
        out_specs=[
            pl.BlockSpec((tb, D_MODEL), lambda i: (i, 0)),
            state_spec,
            pl.BlockSpec((tb, GM_WIDTH), lambda i: (i, 0)),
        ],
        out_shape=[
            jax.ShapeDtypeStruct((nb, D_MODEL), F32),
            jax.ShapeDtypeStruct(state.shape, F32),
            jax.ShapeDtypeStruct((nb, GM_WIDTH), F32),
        ],
        compiler_params=_params(("arbitrary",)),
        name="sample_mixer",
    )(z, state, cosf, sinf, dmask, qdec, kdec, cdec, lng, lnb, w00, bs0)


def _attn_out_kernel(mix_ref, w32_ref, h_ref, g_ref, mixs_ref, hs_ref,
                     h1_ref, xn_ref, h1s_ref, xns_ref, w_ref, *, row_block, row_chunk):
    @pl.when(pl.program_id(0) == 0)
    def _():
        _round_weight(w32_ref, w_ref)
        h1s = hs_ref[...] + jnp.dot(mixs_ref[...].astype(BF16), w_ref[...],
                                    preferred_element_type=F32)
        h1s_ref[...] = h1s
        xns_ref[...] = _rms_rows(h1s, g_ref[...]).astype(BF16)

    tm = h_ref.shape[0]
    for r_lo in range(0, tm, row_block):
        rows = slice(r_lo, r_lo + row_block)
        h1_ref[rows, :] = h_ref[rows, :] + jnp.dot(mix_ref[rows, :], w_ref[...],
                                                   preferred_element_type=F32)
        for r0 in range(r_lo, r_lo + row_block, row_chunk):
            sub = slice(r0, r0 + row_chunk)
            xn_ref[sub, :] = _rms_rows(h1_ref[sub, :], g_ref[...]).astype(BF16)


def _attn_out(mix, w, h, g, mix_s, h_s, *, tm):
    m, d = h.shape
    nb = h_s.shape[0]
    once = pl.Buffered(1)
    const = lambda i: (0, 0)
    tile = pl.BlockSpec((tm, d), lambda i: (i, 0))
    return pl.pallas_call(
        functools.partial(_attn_out_kernel, row_block=256, row_chunk=32),
        grid=(m // tm,),
        in_specs=[
            tile,
            pl.BlockSpec((d, d), const, pipeline_mode=once),
            tile,
            pl.BlockSpec((1, d), const, pipeline_mode=once),
            pl.BlockSpec((nb, d), const, pipeline_mode=once),
            pl.BlockSpec((nb, d), const, pipeline_mode=once),
        ],
        out_specs=[tile, tile, pl.BlockSpec((nb, d), const), pl.BlockSpec((nb, d), const)],
        out_shape=[
            jax.ShapeDtypeStruct((m, d), F32),
            jax.ShapeDtypeStruct((m, d), BF16),
            jax.ShapeDtypeStruct((nb, d), F32),
            jax.ShapeDtypeStruct((nb, d), BF16),
        ],
        scratch_shapes=[pltpu.VMEM((d, d), BF16)],
        compiler_params=_params(("arbitrary",)),
        name="attn_out",
    )(mix, w, h, g, mix_s, h_s)


CONV_PAD = 8


def _ffn_up_kernel(xn_ref, wg32_ref, wu32_ref, cg_ref, cu_ref, xs_ref, sg_state_ref,
                   su_state_ref, act_ref, csg_ref, csu_ref, asg_ref, asu_ref, acts_ref,
                   ag_ref, sg_ref, wg_ref, wu_ref, *au_refs, tiles_per_seq, row_chunk, up_splits):
    i = pl.program_id(1)
    tm, tn = act_ref.shape
    first = (i % tiles_per_seq) == 0
    last_rows = up_splits[-1]

    @pl.when(first)
    def _():
        ag_ref[0:CONV_PAD, :] = jnp.zeros((CONV_PAD, tn), F32)
        au_refs[0][0:CONV_PAD, :] = jnp.zeros((CONV_PAD, tn), F32)

    @pl.when(jnp.logical_not(first))
    def _():
        ag_ref[0:CONV_PAD, :] = ag_ref[tm:tm + CONV_PAD, :]
        au_refs[0][0:CONV_PAD, :] = au_refs[-1][last_rows:last_rows + CONV_PAD, :]

    @pl.when(i == 0)
    def _():
        _round_weight(wg32_ref, wg_ref)
        _round_weight(wu32_ref, wu_ref)
        xs = xs_ref[...]
        nb = xs.shape[0]
        a_g = jnp.dot(xs, wg_ref[...], preferred_element_type=F32)
        a_u = jnp.dot(xs, wu_ref[...], preferred_element_type=F32)
        asg_ref[...] = a_g
        asu_ref[...] = a_u

        def conv_s(state_ref, a, c_ref):
            t = lambda v: v.reshape(nb // 8, 8, tn)
            return (t(state_ref[:, 0, :]) * c_ref[0] + t(state_ref[:, 1, :]) * c_ref[1]
                    + t(a) * c_ref[2] + c_ref[3]).reshape(nb, tn)
        acts_ref[...] = (_silu(conv_s(sg_state_ref, a_g, cg_ref))
                         * conv_s(su_state_ref, a_u, cu_ref)).astype(acts_ref.dtype)

    def conv(a_ref, c_ref, r0):
        def window(back):
            lo = r0 + CONV_PAD - back
            return a_ref[lo:lo + row_chunk, :].reshape(row_chunk // 8, 8, tn)
        return window(2) * c_ref[0] + window(1) * c_ref[1] + window(0) * c_ref[2] + c_ref[3]

    ag_ref[CONV_PAD:CONV_PAD + tm, :] = jnp.dot(xn_ref[...], wg_ref[...],
                                                preferred_element_type=F32)
    for r0 in range(0, tm, row_chunk):
        gate = conv(ag_ref, cg_ref, r0).reshape(row_chunk, tn).astype(BF16)
        sg_ref[r0:r0 + row_chunk, :] = _silu(gate)
    start = 0
    for idx, rows in enumerate(up_splits):
        au_ref = au_refs[idx]
        au_ref[CONV_PAD:CONV_PAD + rows, :] = jnp.dot(xn_ref[start:start + rows, :], wu_ref[...],
                                                      preferred_element_type=F32)
        if idx + 1 < len(up_splits):
            au_refs[idx + 1][0:CONV_PAD, :] = au_ref[rows:rows + CONV_PAD, :]
        start += rows
    start = 0
    for idx, rows in enumerate(up_splits):
        for r0 in range(0, rows, row_chunk):
            up = conv(au_refs[idx], cu_ref, r0).reshape(row_chunk, tn)
            out = slice(start + r0, start + r0 + row_chunk)
            act_ref[out, :] = sg_ref[out, :] * up.astype(BF16)
        start += rows

    @pl.when((i % tiles_per_seq) == tiles_per_seq - 1)
    def _():
        csg_ref[0] = ag_ref[CONV_PAD + tm - 2:CONV_PAD + tm, :]
        csu_ref[0] = au_refs[-1][CONV_PAD + last_rows - 2:CONV_PAD + last_rows, :]


def _ffn_up(xn, w_up, conv_taps, xn_s, conv_state, *, batch, seq, tm, tn):
    m, d = xn.shape
    nb = xn_s.shape[0]
    n_col = D_FF // tn
    tiles_per_seq = seq // tm
    up_splits = (tm // 2, tm // 2)
    sample_cols = lambda k: pl.BlockSpec((nb, 2, tn), lambda j, i: (0, 0, k * n_col + j))
    sample_out = pl.BlockSpec((nb, tn), lambda j, i: (0, j))
    return pl.pallas_call(
        functools.partial(_ffn_up_kernel, tiles_per_seq=tiles_per_seq, row_chunk=64,
                          up_splits=up_splits),
        grid=(n_col, m // tm),
        in_specs=[
            pl.BlockSpec((tm, d), lambda j, i: (i, 0)),
            pl.BlockSpec((d, tn), lambda j, i: (0, j)),
            pl.BlockSpec((d, tn), lambda j, i: (0, n_col + j)),
            pl.BlockSpec((4, 8, tn), lambda j, i: (0, 0, j)),
            pl.BlockSpec((4, 8, tn), lambda j, i: (0, 0, n_col + j)),
            pl.BlockSpec((nb, d), lambda j, i: (0, 0)),
            sample_cols(0), sample_cols(1),
        ],
        out_specs=[
            pl.BlockSpec((tm, tn), lambda j, i: (i, j)),
            pl.BlockSpec((1, 2, tn), lambda j, i: (i // tiles_per_seq, 0, j)),
            pl.BlockSpec((1, 2, tn), lambda j, i: (i // tiles_per_seq, 0, j)),
            sample_out, sample_out, sample_out,
        ],
        out_shape=[
            jax.ShapeDtypeStruct((m, D_FF), BF16),
            jax.ShapeDtypeStruct((batch, 2, D_FF), F32),
            jax.ShapeDtypeStruct((batch, 2, D_FF), F32),
            jax.ShapeDtypeStruct((nb, D_FF), F32),
            jax.ShapeDtypeStruct((nb, D_FF), F32),
            jax.ShapeDtypeStruct((nb, D_FF), BF16),
        ],
        scratch_shapes=[pltpu.VMEM((tm + CONV_PAD, tn), F32),
                        pltpu.VMEM((tm, tn), BF16),
                        pltpu.VMEM((d, tn), BF16),
                        pltpu.VMEM((d, tn), BF16)]
                       + [pltpu.VMEM((rows + CONV_PAD, tn), F32) for rows in up_splits],
        compiler_params=_params(("arbitrary", "arbitrary")),
        name="ffn_up",
    )(xn, w_up, w_up, conv_taps, conv_taps, xn_s, conv_state, conv_state)


def _ffn_down_kernel(act_ref, w32_ref, h_ref, acts_ref, hs_ref, o_ref, os_ref, w_ref):
    @pl.when(pl.program_id(1) == 0)
    def _():
        _round_weight(w32_ref, w_ref)
        os_ref[...] = hs_ref[...] + jnp.dot(acts_ref[...], w_ref[...], preferred_element_type=F32)

    o_ref[...] = h_ref[...] + jnp.dot(act_ref[...], w_ref[...], preferred_element_type=F32)


def _ffn_down(act, w_down, h1, act_s, h1_s, *, tm, tn):
    m, k = act.shape
    n = w_down.shape[1]
    nb = act_s.shape[0]
    return pl.pallas_call(
        _ffn_down_kernel,
        grid=(n // tn, m // tm),
        in_specs=[
            pl.BlockSpec((tm, k), lambda j, i: (i, 0)),
            pl.BlockSpec((k, tn), lambda j, i: (0, j), pipeline_mode=pl.Buffered(1)),
            pl.BlockSpec((tm, tn), lambda j, i: (i, j)),
            pl.BlockSpec((nb, k), lambda j, i: (0, 0)),
            pl.BlockSpec((nb, tn), lambda j, i: (0, j)),
        ],
        out_specs=[
            pl.BlockSpec((tm, tn), lambda j, i: (i, j)),
            pl.BlockSpec((nb, tn), lambda j, i: (0, j)),
        ],
        out_shape=[
            jax.ShapeDtypeStruct((m, n), F32),
            jax.ShapeDtypeStruct((nb, n), F32),
        ],
        scratch_shapes=[pltpu.VMEM((k, tn), BF16)],
        compiler_params=pltpu.CompilerParams(dimension_semantics=("arbitrary", "arbitrary"),
                                             vmem_limit_bytes=FFN_DOWN_VMEM_LIMIT),
        name="ffn_down",
    )(act, w_down, h1, act_s, h1_s)


def _ple_rows(h_ref, p_ref, gple_ref, wg_ref, wp_ref, gfin_ref, y_ref, xn_ref, r_lo, n_rows,
              row_chunk):
    rows = slice(r_lo, r_lo + n_rows)
    for r0 in range(r_lo, r_lo + n_rows, row_chunk):
        sub = slice(r0, r0 + row_chunk)
        xn_ref[sub, :] = _rms_rows(h_ref[sub, :], gple_ref[...]).astype(BF16)
    gate = jax.nn.sigmoid(jnp.dot(xn_ref[rows, :], wg_ref[...], preferred_element_type=F32))
    proj = jnp.dot(p_ref[rows, :].astype(BF16), wp_ref[...], preferred_element_type=F32)
    y_ref[rows, :] = h_ref[rows, :] + proj * gate
    for r0 in range(r_lo, r_lo + n_rows, row_chunk):
        sub = slice(r0, r0 + row_chunk)
        y_ref[sub, :] = _rms_rows(y_ref[sub, :], gfin_ref[...])


def _ple_final_kernel(h_ref, p_ref, gple_ref, wg32_ref, wp32_ref, gfin_ref, hs_ref, ps_ref,
                      y_ref, ys_ref, xn_ref, wg_ref, wp_ref, *, row_block, row_chunk):
    @pl.when(pl.program_id(0) == 0)
    def _():
        _round_weight(wg32_ref, wg_ref)
        _round_weight(wp32_ref, wp_ref)
        _ple_rows(hs_ref, ps_ref, gple_ref, wg_ref, wp_ref, gfin_ref, ys_ref, xn_ref, 0,
                  hs_ref.shape[0], row_chunk)

    for r_lo in range(0, h_ref.shape[0], row_block):
        _ple_rows(h_ref, p_ref, gple_ref, wg_ref, wp_ref, gfin_ref, y_ref, xn_ref, r_lo, row_block,
                  row_chunk)


def _ple_final(h, p, g_ple, w_gate, w_proj, g_final, h_s, p_s, *, tm):
    m, d = h.shape
    pd = p.shape[1]
    nb = h_s.shape[0]
    assert nb <= tm
    once = pl.Buffered(1)
    const = lambda i: (0, 0)
    return pl.pallas_call(
        functools.partial(_ple_final_kernel, row_block=256, row_chunk=32),
        grid=(m // tm,),
        in_specs=[
            pl.BlockSpec((tm, d), lambda i: (i, 0)),
            pl.BlockSpec((tm, pd), lambda i: (i, 0)),
            pl.BlockSpec((1, d), const, pipeline_mode=once),
            pl.BlockSpec((d, d), const, pipeline_mode=once),
            pl.BlockSpec((pd, d), const, pipeline_mode=once),
            pl.BlockSpec((1, d), const, pipeline_mode=once),
            pl.BlockSpec((nb, d), const, pipeline_mode=once),
            pl.BlockSpec((nb, pd), const, pipeline_mode=once),
        ],
        out_specs=[pl.BlockSpec((tm, d), lambda i: (i, 0)), pl.BlockSpec((nb, d), const)],
        out_shape=[jax.ShapeDtypeStruct((m, d), F32), jax.ShapeDtypeStruct((nb, d), F32)],
        scratch_shapes=[pltpu.VMEM((tm, d), BF16), pltpu.VMEM((d, d), BF16),
                        pltpu.VMEM((pd, d), BF16)],
        compiler_params=_params(("arbitrary",)),
        name="ple_final",
    )(h, p, g_ple, w_gate, w_proj, g_final, h_s, p_s)


def _rope_tables(pos):
    inv = ROPE_THETA ** (-np.arange(0, HEAD_DIM, 2, dtype=np.float64) / HEAD_DIM)
    ang = np.asarray(pos, np.float64)[:, None] * inv[None, :]
    cos, sin = np.cos(ang), np.sin(ang)
    return (jnp.asarray(np.concatenate([cos, cos], axis=-1), F32),
            jnp.asarray(np.concatenate([-sin, sin], axis=-1), F32))


def _decay_tables(c):
    log_g = np.log1p(-np.exp2(-5.0 - np.arange(RET_HEADS, dtype=np.float64)))
    idx = np.arange(c, dtype=np.float64)
    diff = idx[:, None] - idx[None, :]
    dmask = np.where(diff >= 0, np.exp(log_g[:, None, None] * np.maximum(diff, 0.0)), 0.0)
    q_dec = np.exp(log_g[:, None] * (idx + 1.0))[..., None]
    k_dec = np.exp(log_g[:, None] * (c - 1.0 - idx))[..., None]
    c_dec = np.exp(log_g * c)[:, None, None]
    return tuple(jnp.asarray(t, F32) for t in (dmask, q_dec, k_dec, c_dec))


def kernel(x_prompt, x_sample, p_prompt, p_sample, state_ret, state_conv, g_attn, w_in, gm_ln_g,
           gm_ln_b, gm_ws, gm_bs, w_o, g_ffn, w_up, conv_w, conv_b, w_down, g_ple, w_ple_gate,
           w_ple_proj, g_final):
    batch, seq, d = x_prompt.shape
    nb = x_sample.shape[0]
    assert x_sample.shape[1] == 1 and g_attn.shape[0] == 1

    w_in_b = w_in[0].astype(BF16)

    g_attn2, g_ffn2, g_ple2 = g_attn[0][None], g_ffn[0][None], g_ple[0][None]
    g_fin2 = g_final[None]
    lng, lnb = gm_ln_g[0][None], gm_ln_b[0][None]
    ws, bs = gm_ws[0], gm_bs[0]
    cw, cb = conv_w[0], conv_b[0][None]
    n_groups = GM_WIDTH // HEAD_DIM

    cos_p, sin_p = _rope_tables(np.arange(seq))
    dmask, q_dec, k_dec, c_dec = _decay_tables(CHUNK)
    qdec_p = jnp.broadcast_to(q_dec, (RET_HEADS, CHUNK, HEAD_DIM))
    kdec_p = jnp.broadcast_to(k_dec, (RET_HEADS, CHUNK, HEAD_DIM))
    cdec_p = jnp.broadcast_to(c_dec, (RET_HEADS, 1, HEAD_DIM))
    bsf = jnp.broadcast_to(bs.T[:, :, None], (CHUNK, n_groups, HEAD_DIM)).reshape(CHUNK, GM_WIDTH)
    cos_s, sin_s = _rope_tables(PAST_LEN + np.arange(1))
    dmask1, q_dec1, k_dec1, c_dec1 = _decay_tables(1)
    qdec_s = jnp.broadcast_to(q_dec1, (RET_HEADS, 1, HEAD_DIM))
    kdec_s = jnp.broadcast_to(k_dec1, (RET_HEADS, 1, HEAD_DIM))
    cdec_s = jnp.broadcast_to(c_dec1, (RET_HEADS, 1, HEAD_DIM))
    w00 = jnp.broadcast_to(ws[:, 0, 0][:, None], (n_groups, HEAD_DIM)).reshape(1, GM_WIDTH)
    bs0 = jnp.broadcast_to(bs[:, 0][:, None], (n_groups, HEAD_DIM)).reshape(1, GM_WIDTH)
    conv_taps = jnp.broadcast_to(jnp.concatenate([cw, cb], axis=0)[:, None, :], (4, 8, 2 * D_FF))

    xp = x_prompt.reshape(batch * seq, d)
    xs = x_sample.reshape(nb, d)

    mix_p, ret_p = _in_mixer(xp, g_attn2, w_in_b, batch, seq, cos_p, sin_p, dmask, qdec_p, kdec_p,
                             cdec_p, lng, lnb, ws, bsf, tm=512)
    z_s = _norm_matmul(xs, g_attn2, w_in_b, tm=nb, tn=1024, name="sample_in_proj")
    mix_s, ret_s, vn_s = _sample_mixer(z_s, state_ret[0], cos_s, sin_s, dmask1, qdec_s, kdec_s,
                                       cdec_s, lng, lnb, w00, bs0, tb=8)
    h1_p, xn2_p, h1_s, xn2_s = _attn_out(mix_p, w_o[0], xp, g_ffn2, mix_s, xs, tm=512)

    act_p, csg_p, csu_p, asg_s, asu_s, act_s = _ffn_up(
        xn2_p, w_up[0], conv_taps, xn2_s, state_conv[0], batch=batch, seq=seq, tm=1024, tn=512)
    h2_p, h2_s = _ffn_down(act_p, w_down[0], h1_p, act_s, h1_s, tm=512, tn=1024)

    y_p, y_s = _ple_final(h2_p, p_prompt[0].reshape(batch * seq, PLE_DIM), g_ple2, w_ple_gate[0],
                          w_ple_proj[0], g_fin2, h2_s, p_sample[0].reshape(nb, PLE_DIM), tm=512)

    conv_p = jnp.concatenate([csg_p, csu_p], axis=-1)[None]
    a_s = jnp.concatenate([asg_s, asu_s], axis=-1)
    conv_s = jnp.stack([state_conv[0][:, 1, :], a_s], axis=1)[None]
    return (y_p.reshape(batch, seq, d), y_s.reshape(nb, 1, d), ret_p[None], conv_p,
            ret_s[None], conv_s, vn_s.reshape(1, nb, 1, GM_WIDTH))
```

```python
import functools
import math

import jax
import jax.numpy as jnp
import numpy as np
from jax import lax
from jax.experimental import pallas as pl
from jax.experimental.pallas import tpu as pltpu

F32 = jnp.float32
BF16 = jnp.bfloat16

D_MODEL = 2048
RET_HEADS = 8
HEAD_DIM = 128
CHUNK = 128
RET_QK = RET_HEADS * HEAD_DIM
GM_WIDTH = 1024
IN_COLS = 6144
D_FF = 5632
PLE_DIM = 256
ROPE_THETA = 10000.0
PAST_LEN = 16384
EPS = 1e-6
K_SCALE = HEAD_DIM ** -0.5

OFF_Q, OFF_K, OFF_V, OFF_G, OFF_U, OFF_VG = 0, 1024, 2048, 3072, 4096, 5120

VMEM_LIMIT = 56 * 1024 * 1024
LANES = 128
IN_MIXER_VMEM_LIMIT = 62 * 1024 * 1024
FFN_DOWN_VMEM_LIMIT = 60 * 1024 * 1024
WEIGHT_CAST_ROWS = 256


def _params(semantics):
    return pltpu.CompilerParams(dimension_semantics=semantics, vmem_limit_bytes=VMEM_LIMIT)


def _rms_rows(x, g):
    ms = jnp.mean(x * x, axis=-1, keepdims=True)
    return x * lax.rsqrt(ms + EPS) * g


def _gelu(x):
    return jax.nn.gelu(x)


def _silu(x):
    return x * jax.nn.sigmoid(x)


def _round_weight(w32_ref, w_ref):
    for k0 in range(0, w_ref.shape[0], WEIGHT_CAST_ROWS):
        rows = slice(k0, min(k0 + WEIGHT_CAST_ROWS, w_ref.shape[0]))
        w_ref[rows, :] = w32_ref[rows, :].astype(BF16)


def _norm_matmul_kernel(x_ref, g_ref, w32_ref, o_ref, w_ref, xn_ref, *, row_chunk):
    @pl.when(pl.program_id(0) == 0)
    def _():
        def body(r, carry):
            rows = pl.ds(pl.multiple_of(r * row_chunk, row_chunk), row_chunk)
            xn_ref[rows, :] = _rms_rows(x_ref[rows, :], g_ref[...]).astype(BF16)
            return carry
        lax.fori_loop(0, x_ref.shape[0] // row_chunk, body, 0)

    _round_weight(w32_ref, w_ref)
    o_ref[...] = jnp.dot(xn_ref[...], w_ref[...], preferred_element_type=F32)


def _norm_matmul(x, g, w, *, tn, name):
    m, k = x.shape
    n = w.shape[1]
    return pl.pallas_call(
        functools.partial(_norm_matmul_kernel, row_chunk=32),
        grid=(n // tn,),
        in_specs=[
            pl.BlockSpec((m, k), lambda j: (0, 0)),
            pl.BlockSpec((1, k), lambda j: (0, 0)),
            pl.BlockSpec((k, tn), lambda j: (0, j)),
        ],
        out_specs=[pl.BlockSpec((m, tn), lambda j: (0, j)), pl.BlockSpec((k, tn), lambda j: (0, j))],
        out_shape=[jax.ShapeDtypeStruct((m, n), F32), jax.ShapeDtypeStruct((k, n), BF16)],
        scratch_shapes=[pltpu.VMEM((m, k), BF16)],
        compiler_params=_params(("arbitrary",)),
        name=name,
    )(x, g, w)


def _rope(x, cosf, sinf):
    return x * cosf + pltpu.roll(x, HEAD_DIM // 2, 1) * sinf


def _layernorm_rows(x, g, b):
    mu = jnp.mean(x, axis=-1, keepdims=True)
    xc = x - mu
    var = jnp.mean(xc * xc, axis=-1, keepdims=True)
    return xc * lax.rsqrt(var + EPS) * g + b


def _retention_heads(z_ref, r0, heads, cosf, sinf, dmask_ref, qdec_ref, kdec_ref, cdec_ref,
                     mix_ref, out_r0, state_refs):
    zr = slice(r0, r0 + CHUNK)
    orows = slice(out_r0, out_r0 + CHUNK)
    staged = []
    for h in heads:
        q = _rope(z_ref[zr, OFF_Q + h * HEAD_DIM:OFF_Q + (h + 1) * HEAD_DIM], cosf, sinf)
        k = _rope(z_ref[zr, OFF_K + h * HEAD_DIM:OFF_K + (h + 1) * HEAD_DIM], cosf, sinf) * K_SCALE
        v = z_ref[zr, OFF_V + h * HEAD_DIM:OFF_V + (h + 1) * HEAD_DIM].astype(BF16)
        s_old = state_refs[h][...]
        sc = lax.dot_general(q.astype(BF16), k.astype(BF16), (((1,), (1,)), ((), ())),
                             preferred_element_type=F32) * dmask_ref[h]
        o_cross = jnp.dot((q * qdec_ref[h]).astype(BF16), s_old.astype(BF16),
                          preferred_element_type=F32)
        state_refs[h][...] = s_old * cdec_ref[h] + lax.dot_general(
            (k * kdec_ref[h]).astype(BF16), v, (((0,), (0,)), ((), ())),
            preferred_element_type=F32)
        staged.append((h, sc, o_cross, v))
    for h, sc, o_cross, v in staged:
        o = jnp.dot(sc.astype(BF16), v, preferred_element_type=F32) + o_cross
        o = o * lax.rsqrt(jnp.mean(o * o, axis=-1, keepdims=True) + EPS)
        g = z_ref[zr, OFF_G + h * HEAD_DIM:OFF_G + (h + 1) * HEAD_DIM]
        mix_ref[orows, h * HEAD_DIM:(h + 1) * HEAD_DIM] = (o * _silu(g)).astype(mix_ref.dtype)


def _gating_norm(z_ref, r0, lng_ref, lnb_ref, vn_ref):
    zr = slice(r0, r0 + CHUNK)
    vn = _layernorm_rows(_gelu(z_ref[zr, OFF_VG:OFF_VG + GM_WIDTH]), lng_ref[...], lnb_ref[...])
    vn_ref[...] = vn.astype(BF16)


def _gating_groups(z_ref, r0, groups, vn_ref, wm_ref, bsf_ref, mix_ref, out_r0):
    zr = slice(r0, r0 + CHUNK)
    orows = slice(out_r0, out_r0 + CHUNK)
    for grp in groups:
        cols = slice(grp * HEAD_DIM, (grp + 1) * HEAD_DIM)
        sp = jnp.dot(wm_ref[grp], vn_ref[:, cols], preferred_element_type=F32) + bsf_ref[:, cols]
        u = _gelu(z_ref[zr, OFF_U + grp * HEAD_DIM:OFF_U + (grp + 1) * HEAD_DIM])
        mix_ref[orows, RET_QK + grp * HEAD_DIM:RET_QK + (grp + 1) * HEAD_DIM] = (
            (u * sp).astype(mix_ref.dtype))


N_SLOTS = 8


def _in_mixer_kernel(x_ref, g_ref, w_ref, cos_ref, sin_ref, dmask_ref, qdec_ref, kdec_ref, cdec_ref,
                     lng_ref, lnb_ref, ws_ref, bsf_ref, mix_ref, s_ref,
                     xn_a, xn_b, z_a, z_b, wm_ref, vn_0, vn_1, vn_2, vn_3, *state_refs,
                     steps_per_seq, row_chunk):
    step = pl.program_id(0)
    n_groups = GM_WIDTH // HEAD_DIM

    @pl.when(step % steps_per_seq == 0)
    def _():
        for state_ref in state_refs:
            state_ref[...] = jnp.zeros_like(state_ref)

    @pl.when(step == 0)
    def _():
        row = lax.broadcasted_iota(jnp.int32, (CHUNK, CHUNK), 0)
        col = lax.broadcasted_iota(jnp.int32, (CHUNK, CHUNK), 1)
        for grp in range(n_groups):
            wm_ref[grp] = jnp.where(row >= col, ws_ref[grp], 0.0).astype(BF16)

    half = x_ref.shape[0] // 2
    assert half == 2 * CHUNK and RET_HEADS == N_SLOTS and n_groups == N_SLOTS

    def norm_rows(xn_ref, base):
        for r0 in range(0, half, row_chunk):
            xn_ref[r0:r0 + row_chunk, :] = _rms_rows(
                x_ref[base + r0:base + r0 + row_chunk, :], g_ref[...]).astype(BF16)

    def heads(z_ref, c0, base, hs):
        pos = slice(base + c0, base + c0 + CHUNK)
        _retention_heads(z_ref, c0, hs, cos_ref[pos, :], sin_ref[pos, :], dmask_ref, qdec_ref,
                         kdec_ref, cdec_ref, mix_ref, base + c0, state_refs)

    def groups(z_ref, c0, base, vn_ref, gs):
        _gating_groups(z_ref, c0, gs, vn_ref, wm_ref, bsf_ref, mix_ref, base + c0)

    norm_rows(xn_a, 0)
    z_a[...] = jnp.dot(xn_a[...], w_ref[...], preferred_element_type=F32)
    norm_rows(xn_b, half)

    piece = w_ref.shape[1] // N_SLOTS
    first_gating_piece = OFF_U // piece
    order = list(range(first_gating_piece, N_SLOTS)) + list(range(first_gating_piece))
    gating_ready = N_SLOTS - first_gating_piece
    assert gating_ready + 4 <= N_SLOTS
    xnb = xn_b[...]
    for slot in range(N_SLOTS):
        cols = slice(order[slot] * piece, (order[slot] + 1) * piece)
        z_b[:, cols] = jnp.dot(xnb, w_ref[:, cols], preferred_element_type=F32)
        c0 = (slot // 4) * CHUNK
        if slot == 0:
            _gating_norm(z_a, 0, lng_ref, lnb_ref, vn_0)
        if slot == 3:
            _gating_norm(z_a, CHUNK, lng_ref, lnb_ref, vn_1)
        heads(z_a, c0, 0, (2 * (slot % 4), 2 * (slot % 4) + 1))
        if slot in (1, 2):
            groups(z_a, 0, 0, vn_0, range(4 * (slot - 1), 4 * slot))
        if slot in (4, 5):
            groups(z_a, CHUNK, 0, vn_1, range(4 * (slot - 4), 4 * (slot - 3)))
        late = slot - gating_ready
        if 0 <= late < 4:
            b_c0, b_vn = ((0, vn_2), (CHUNK, vn_3))[late // 2]
            if late % 2 == 0:
                _gating_norm(z_b, b_c0, lng_ref, lnb_ref, b_vn)
            groups(z_b, b_c0, half, b_vn, range(4 * (late % 2), 4 * (late % 2) + 4))

    for c0 in (0, CHUNK):
        heads(z_b, c0, half, range(RET_HEADS))
    for h, state_ref in enumerate(state_refs):
        s_ref[0, h] = state_ref[...]


def _in_mixer(x, g, w, batch, seq, cosf, sinf, dmask, qdec, kdec, cdec, lng, lnb, ws, bsf, *, tm):
    m, d = x.shape
    n = w.shape[1]
    steps_per_seq = seq // tm
    n_groups = GM_WIDTH // HEAD_DIM
    once = pl.Buffered(1)
    const3 = lambda s: (0, 0, 0)
    const2 = lambda s: (0, 0)
    return pl.pallas_call(
        functools.partial(_in_mixer_kernel, steps_per_seq=steps_per_seq, row_chunk=32),
        grid=(m // tm,),
        in_specs=[
            pl.BlockSpec((tm, d), lambda s: (s, 0)),
            pl.BlockSpec((1, d), const2, pipeline_mode=once),
            pl.BlockSpec((d, n), const2, pipeline_mode=once),
            pl.BlockSpec((tm, HEAD_DIM), lambda s: (s % steps_per_seq, 0)),
            pl.BlockSpec((tm, HEAD_DIM), lambda s: (s % steps_per_seq, 0)),
            pl.BlockSpec((RET_HEADS, CHUNK, CHUNK), const3, pipeline_mode=once),
            pl.BlockSpec((RET_HEADS, CHUNK, HEAD_DIM), const3, pipeline_mode=once),
            pl.BlockSpec((RET_HEADS, CHUNK, HEAD_DIM), const3, pipeline_mode=once),
            pl.BlockSpec((RET_HEADS, 1, HEAD_DIM), const3, pipeline_mode=once),
            pl.BlockSpec((1, GM_WIDTH), const2, pipeline_mode=once),
            pl.BlockSpec((1, GM_WIDTH), const2, pipeline_mode=once),
            pl.BlockSpec((n_groups, CHUNK, CHUNK), const3, pipeline_mode=once),
            pl.BlockSpec((CHUNK, GM_WIDTH), const2, pipeline_mode=once),
        ],
        out_specs=[
            pl.BlockSpec((tm, D_MODEL), lambda s: (s, 0)),
            pl.BlockSpec((1, RET_HEADS, HEAD_DIM, HEAD_DIM), lambda s: (s // steps_per_seq, 0, 0, 0)),
        ],
        out_shape=[
            jax.ShapeDtypeStruct((m, D_MODEL), BF16),
            jax.ShapeDtypeStruct((batch, RET_HEADS, HEAD_DIM, HEAD_DIM), F32),
        ],
        scratch_shapes=[
            pltpu.VMEM((tm // 2, d), BF16), pltpu.VMEM((tm // 2, d), BF16),
            pltpu.VMEM((tm // 2, n), F32), pltpu.VMEM((tm // 2, n), F32),
            pltpu.VMEM((n_groups, CHUNK, CHUNK), BF16),
        ] + [pltpu.VMEM((CHUNK, GM_WIDTH), BF16) for _ in range(4)]
          + [pltpu.VMEM((HEAD_DIM, HEAD_DIM), F32) for _ in range(RET_HEADS)],
        compiler_params=pltpu.CompilerParams(dimension_semantics=("arbitrary",),
                                             vmem_limit_bytes=IN_MIXER_VMEM_LIMIT),
        name="prompt_in_mixer",
    )(x, g, w, cosf, sinf, dmask, qdec, kdec, cdec, lng, lnb, ws, bsf)


def _sample_mixer_kernel(z_ref, s_ref, cos_ref, sin_ref, dmask_ref, qdec_ref, kdec_ref, cdec_ref,
                         lng_ref, lnb_ref, w00_ref, bs0_ref, mix_ref, so_ref, vn_ref, *, tb):
    cosf = cos_ref[...]
    sinf = sin_ref[...]
    row = lax.broadcasted_iota(jnp.int32, (HEAD_DIM, HEAD_DIM), 0)
    col = lax.broadcasted_iota(jnp.int32, (HEAD_DIM, HEAD_DIM), 1)
    eye = row == col
    ones = jnp.ones((HEAD_DIM, HEAD_DIM), BF16)

    def lane_broadcast_columns(rows):
        diag = jnp.concatenate(
            [jnp.where(eye, jnp.broadcast_to(rows[i:i + 1, :], (HEAD_DIM, HEAD_DIM)), 0.0)
             for i in range(rows.shape[0])], axis=0)
        return jnp.dot(diag.astype(BF16), ones, preferred_element_type=F32)

    for h in range(RET_HEADS):
        cols = slice(h * HEAD_DIM, (h + 1) * HEAD_DIM)
        q = _rope(z_ref[:, OFF_Q + h * HEAD_DIM:OFF_Q + (h + 1) * HEAD_DIM], cosf, sinf)
        k = _rope(z_ref[:, OFF_K + h * HEAD_DIM:OFF_K + (h + 1) * HEAD_DIM], cosf, sinf) * K_SCALE
        v = z_ref[:, OFF_V + h * HEAD_DIM:OFF_V + (h + 1) * HEAD_DIM]
        g = z_ref[:, OFF_G + h * HEAD_DIM:OFF_G + (h + 1) * HEAD_DIM]
        qdec = qdec_ref[h]
        kdec = kdec_ref[h]
        cdec = cdec_ref[h]
        sc = jnp.sum(q * k, axis=-1, keepdims=True) * dmask_ref[h]
        qk_cols = lane_broadcast_columns(jnp.concatenate([q * qdec, k * kdec], axis=0))
        o_rows = []
        for b in range(tb):
            s_old = s_ref[b, h]
            q_col = qk_cols[b * HEAD_DIM:(b + 1) * HEAD_DIM, :]
            k_col = qk_cols[(tb + b) * HEAD_DIM:(tb + b + 1) * HEAD_DIM, :]
            v_row = v[b:b + 1, :]
            o_rows.append(jnp.sum(q_col * s_old, axis=0, keepdims=True))
            so_ref[b, h] = s_old * cdec + k_col * v_row
        o = sc * v + jnp.concatenate(o_rows, axis=0)
        o = o * lax.rsqrt(jnp.mean(o * o, axis=-1, keepdims=True) + EPS)
        mix_ref[:, cols] = o * _silu(g)

    vn = _layernorm_rows(_gelu(z_ref[:, OFF_VG:OFF_VG + GM_WIDTH]), lng_ref[...], lnb_ref[...])
    vn_ref[...] = vn
    u = _gelu(z_ref[:, OFF_U:OFF_U + GM_WIDTH])
    mix_ref[:, RET_QK:RET_QK + GM_WIDTH] = u * (w00_ref[...] * vn + bs0_ref[...])


def _sample_mixer(z, state, cosf, sinf, dmask, qdec, kdec, cdec, lng, lnb, w00, bs0, *, tb):
    nb = z.shape[0]
    const3 = lambda i: (0, 0, 0)
    const2 = lambda i: (0, 0)
    state_spec = pl.BlockSpec((tb, RET_HEADS, HEAD_DIM, HEAD_DIM), lambda i: (i, 0, 0, 0))
    return pl.pallas_call(
        functools.partial(_sample_mixer_kernel, tb=tb),
        grid=(nb // tb,),
        in_specs=[
            pl.BlockSpec((tb, IN_COLS), lambda i: (i, 0)),
            state_spec,
            pl.BlockSpec((1, HEAD_DIM), const2),
            pl.BlockSpec((1, HEAD_DIM), const2),
            pl.BlockSpec((RET_HEADS, 1, 1), const3),
            pl.BlockSpec((RET_HEADS, 1, HEAD_DIM), const3),
            pl.BlockSpec((RET_HEADS, 1, HEAD_DIM), const3),
            pl.BlockSpec((RET_HEADS, 1, HEAD_DIM), const3),
            pl.BlockSpec((1, GM_WIDTH), const2),
            pl.BlockSpec((1, GM_WIDTH), const2),
            pl.BlockSpec((1, GM_WIDTH), const2),
            pl.BlockSpec((1, GM_WIDTH), const2),
        ],
        out_specs=[
            pl.BlockSpec((tb, D_MODEL), lambda i: (i, 0)),
            state_spec,
            pl.BlockSpec((tb, GM_WIDTH), lambda i: (i, 0)),
        ],
        out_shape=[
            jax.ShapeDtypeStruct((nb, D_MODEL), F32),
            jax.ShapeDtypeStruct(state.shape, F32),
            jax.ShapeDtypeStruct((nb, GM_WIDTH), F32),
        ],
        compiler_params=_params(("arbitrary",)),
        name="sample_mixer",
    )(z, state, cosf, sinf, dmask, qdec, kdec, cdec, lng, lnb, w00, bs0)


def _attn_out_kernel(mix_ref, w32_ref, h_ref, g_ref, mixs_ref, hs_ref,
                     h1_ref, xn_ref, h1s_ref, xns_ref, w_ref, *, row_block, row_chunk):
    @pl.when(pl.program_id(0) == 0)
    def _():
        _round_weight(w32_ref, w_ref)
        h1s = hs_ref[...] + jnp.dot(mixs_ref[...].astype(BF16), w_ref[...],
                                    preferred_element_type=F32)
        h1s_ref[...] = h1s
        xns_ref[...] = _rms_rows(h1s, g_ref[...]).astype(BF16)

    tm = h_ref.shape[0]
    for r_lo in range(0, tm, row_block):
        rows = slice(r_lo, r_lo + row_block)
        h1_ref[rows, :] = h_ref[rows, :] + jnp.dot(mix_ref[rows, :], w_ref[...],
                                                   preferred_element_type=F32)
        for r0 in range(r_lo, r_lo + row_block, row_chunk):
            sub = slice(r0, r0 + row_chunk)
            xn_ref[sub, :] = _rms_rows(h1_ref[sub, :], g_ref[...]).astype(BF16)


def _attn_out(mix, w, h, g, mix_s, h_s, *, tm):
    m, d = h.shape
    nb = h_s.shape[0]
    once = pl.Buffered(1)
    const = lambda i: (0, 0)
    tile = pl.BlockSpec((tm, d), lambda i: (i, 0))
    return pl.pallas_call(
        functools.partial(_attn_out_kernel, row_block=256, row_chunk=32),
        grid=(m // tm,),
        in_specs=[
            tile,
            pl.BlockSpec((d, d), const, pipeline_mode=once),
            tile,
            pl.BlockSpec((1, d), const, pipeline_mode=once),
            pl.BlockSpec((nb, d), const, pipeline_mode=once),
            pl.BlockSpec((nb, d), const, pipeline_mode=once),
        ],
        out_specs=[tile, tile, pl.BlockSpec((nb, d), const), pl.BlockSpec((nb, d), const)],
        out_shape=[
            jax.ShapeDtypeStruct((m, d), F32),
            jax.ShapeDtypeStruct((m, d), BF16),
            jax.ShapeDtypeStruct((nb, d), F32),
            jax.ShapeDtypeStruct((nb, d), BF16),
        ],
        scratch_shapes=[pltpu.VMEM((d, d), BF16)],
        compiler_params=_params(("arbitrary",)),
        name="attn_out",
    )(mix, w, h, g, mix_s, h_s)


CONV_PAD = 8


def _ffn_up_kernel(xn_ref, wg32_ref, wu32_ref, cg_ref, cu_ref, xs_ref, sg_state_ref,
                   su_state_ref, act_ref, csg_ref, csu_ref, asg_ref, asu_ref, acts_ref,
                   ag_ref, sg_ref, wg_ref, wu_ref, *au_refs, tiles_per_seq, row_chunk, up_splits):
    i = pl.program_id(1)
    tm, tn = act_ref.shape
    first = (i % tiles_per_seq) == 0
    last_rows = up_splits[-1]

    @pl.when(first)
    def _():
        ag_ref[0:CONV_PAD, :] = jnp.zeros((CONV_PAD, tn), F32)
        au_refs[0][0:CONV_PAD, :] = jnp.zeros((CONV_PAD, tn), F32)

    @pl.when(jnp.logical_not(first))
    def _():
        ag_ref[0:CONV_PAD, :] = ag_ref[tm:tm + CONV_PAD, :]
        au_refs[0][0:CONV_PAD, :] = au_refs[-1][last_rows:last_rows + CONV_PAD, :]

    @pl.when(i == 0)
    def _():
        _round_weight(wg32_ref, wg_ref)
        _round_weight(wu32_ref, wu_ref)
        xs = xs_ref[...]
        nb = xs.shape[0]
        a_g = jnp.dot(xs, wg_ref[...], preferred_element_type=F32)
        a_u = jnp.dot(xs, wu_ref[...], preferred_element_type=F32)
        asg_ref[...] = a_g
        asu_ref[...] = a_u

        def conv_s(state_ref, a, c_ref):
            t = lambda v: v.reshape(nb // 8, 8, tn)
            return (t(state_ref[:, 0, :]) * c_ref[0] + t(state_ref[:, 1, :]) * c_ref[1]
                    + t(a) * c_ref[2] + c_ref[3]).reshape(nb, tn)
        acts_ref[...] = (_silu(conv_s(sg_state_ref, a_g, cg_ref))
                         * conv_s(su_state_ref, a_u, cu_ref)).astype(acts_ref.dtype)

    def conv(a_ref, c_ref, r0):
        def window(back):
            lo = r0 + CONV_PAD - back
            return a_ref[lo:lo + row_chunk, :].reshape(row_chunk // 8, 8, tn)
        return window(2) * c_ref[0] + window(1) * c_ref[1] + window(0) * c_ref[2] + c_ref[3]

    ag_ref[CONV_PAD:CONV_PAD + tm, :] = jnp.dot(xn_ref[...], wg_ref[...],
                                                preferred_element_type=F32)
    for r0 in range(0, tm, row_chunk):
        gate = conv(ag_ref, cg_ref, r0).reshape(row_chunk, tn).astype(BF16)
        sg_ref[r0:r0 + row_chunk, :] = _silu(gate)
    start = 0
    for idx, rows in enumerate(up_splits):
        au_ref = au_refs[idx]
        au_ref[CONV_PAD:CONV_PAD + rows, :] = jnp.dot(xn_ref[start:start + rows, :], wu_ref[...],
                                                      preferred_element_type=F32)
        if idx + 1 < len(up_splits):
            au_refs[idx + 1][0:CONV_PAD, :] = au_ref[rows:rows + CONV_PAD, :]
        start += rows
    start = 0
    for idx, rows in enumerate(up_splits):
        for r0 in range(0, rows, row_chunk):
            up = conv(au_refs[idx], cu_ref, r0).reshape(row_chunk, tn)
            out = slice(start + r0, start + r0 + row_chunk)
            act_ref[out, :] = sg_ref[out, :] * up.astype(BF16)
        start += rows

    @pl.when((i % tiles_per_seq) == tiles_per_seq - 1)
    def _():
        csg_ref[0] = ag_ref[CONV_PAD + tm - 2:CONV_PAD + tm, :]
        csu_ref[0] = au_refs[-1][CONV_PAD + last_rows - 2:CONV_PAD + last_rows, :]


def _ffn_up(xn, w_up, conv_taps, xn_s, conv_state, *, batch, seq, tm, tn):
    m, d = xn.shape
    nb = xn_s.shape[0]
    n_col = D_FF // tn
    tiles_per_seq = seq // tm
    up_splits = (tm // 2, tm // 2)
    sample_cols = lambda k: pl.BlockSpec((nb, 2, tn), lambda j, i: (0, 0, k * n_col + j))
    sample_out = pl.BlockSpec((nb, tn), lambda j, i: (0, j))
    return pl.pallas_call(
        functools.partial(_ffn_up_kernel, tiles_per_seq=tiles_per_seq, row_chunk=64,
                          up_splits=up_splits),
        grid=(n_col, m // tm),
        in_specs=[
            pl.BlockSpec((tm, d), lambda j, i: (i, 0)),
            pl.BlockSpec((d, tn), lambda j, i: (0, j)),
            pl.BlockSpec((d, tn), lambda j, i: (0, n_col + j)),
            pl.BlockSpec((4, 8, tn), lambda j, i: (0, 0, j)),
            pl.BlockSpec((4, 8, tn), lambda j, i: (0, 0, n_col + j)),
            pl.BlockSpec((nb, d), lambda j, i: (0, 0)),
            sample_cols(0), sample_cols(1),
        ],
        out_specs=[
            pl.BlockSpec((tm, tn), lambda j, i: (i, j)),
            pl.BlockSpec((1, 2, tn), lambda j, i: (i // tiles_per_seq, 0, j)),
            pl.BlockSpec((1, 2, tn), lambda j, i: (i // tiles_per_seq, 0, j)),
            sample_out, sample_out, sample_out,
        ],
        out_shape=[
            jax.ShapeDtypeStruct((m, D_FF), BF16),
            jax.ShapeDtypeStruct((batch, 2, D_FF), F32),
            jax.ShapeDtypeStruct((batch, 2, D_FF), F32),
            jax.ShapeDtypeStruct((nb, D_FF), F32),
            jax.ShapeDtypeStruct((nb, D_FF), F32),
            jax.ShapeDtypeStruct((nb, D_FF), BF16),
        ],
        scratch_shapes=[pltpu.VMEM((tm + CONV_PAD, tn), F32),
                        pltpu.VMEM((tm, tn), BF16),
                        pltpu.VMEM((d, tn), BF16),
                        pltpu.VMEM((d, tn), BF16)]
                       + [pltpu.VMEM((rows + CONV_PAD, tn), F32) for rows in up_splits],
        compiler_params=_params(("arbitrary", "arbitrary")),
        name="ffn_up",
    )(xn, w_up, w_up, conv_taps, conv_taps, xn_s, conv_state, conv_state)


def _ffn_down_kernel(act_ref, w32_ref, h_ref, acts_ref, hs_ref, o_ref, os_ref, w_ref):
    @pl.when(pl.program_id(1) == 0)
    def _():
        _round_weight(w32_ref, w_ref)
        os_ref[...] = hs_ref[...] + jnp.dot(acts_ref[...], w_ref[...], preferred_element_type=F32)

    o_ref[...] = h_ref[...] + jnp.dot(act_ref[...], w_ref[...], preferred_element_type=F32)


def _ffn_down(act, w_down, h1, act_s, h1_s, *, tm, tn):
    m, k = act.shape
    n = w_down.shape[1]
    nb = act_s.shape[0]
    return pl.pallas_call(
        _ffn_down_kernel,
        grid=(n // tn, m // tm),
        in_specs=[
            pl.BlockSpec((tm, k), lambda j, i: (i, 0)),
            pl.BlockSpec((k, tn), lambda j, i: (0, j), pipeline_mode=pl.Buffered(1)),
            pl.BlockSpec((tm, tn), lambda j, i: (i, j)),
            pl.BlockSpec((nb, k), lambda j, i: (0, 0)),
            pl.BlockSpec((nb, tn), lambda j, i: (0, j)),
        ],
        out_specs=[
            pl.BlockSpec((tm, tn), lambda j, i: (i, j)),
            pl.BlockSpec((nb, tn), lambda j, i: (0, j)),
        ],
        out_shape=[
            jax.ShapeDtypeStruct((m, n), F32),
            jax.ShapeDtypeStruct((nb, n), F32),
        ],
        scratch_shapes=[pltpu.VMEM((k, tn), BF16)],
        compiler_params=pltpu.CompilerParams(dimension_semantics=("arbitrary", "arbitrary"),
                                             vmem_limit_bytes=FFN_DOWN_VMEM_LIMIT),
        name="ffn_down",
    )(act, w_down, h1, act_s, h1_s)


def _ple_rows(h_ref, p_ref, gple_ref, wg_ref, wp_ref, gfin_ref, y_ref, xn_ref, r_lo, n_rows,
              row_chunk):
    rows = slice(r_lo, r_lo + n_rows)
    for r0 in range(r_lo, r_lo + n_rows, row_chunk):
        sub = slice(r0, r0 + row_chunk)
        xn_ref[sub, :] = _rms_rows(h_ref[sub, :], gple_ref[...]).astype(BF16)
    gate = jax.nn.sigmoid(jnp.dot(xn_ref[rows, :], wg_ref[...], preferred_element_type=F32))
    proj = jnp.dot(p_ref[rows, :].astype(BF16), wp_ref[...], preferred_element_type=F32)
    y_ref[rows, :] = h_ref[rows, :] + proj * gate
    for r0 in range(r_lo, r_lo + n_rows, row_chunk):
        sub = slice(r0, r0 + row_chunk)
        y_ref[sub, :] = _rms_rows(y_ref[sub, :], gfin_ref[...])


def _ple_final_kernel(h_ref, p_ref, gple_ref, wg32_ref, wp32_ref, gfin_ref, hs_ref, ps_ref,
                      y_ref, ys_ref, xn_ref, wg_ref, wp_ref, *, row_block, row_chunk):
    @pl.when(pl.program_id(0) == 0)
    def _():
        _round_weight(wg32_ref, wg_ref)
        _round_weight(wp32_ref, wp_ref)
        _ple_rows(hs_ref, ps_ref, gple_ref, wg_ref, wp_ref, gfin_ref, ys_ref, xn_ref, 0,
                  hs_ref.shape[0], row_chunk)

    for r_lo in range(0, h_ref.shape[0], row_block):
        _ple_rows(h_ref, p_ref, gple_ref, wg_ref, wp_ref, gfin_ref, y_ref, xn_ref, r_lo, row_block,
                  row_chunk)


def _ple_final(h, p, g_ple, w_gate, w_proj, g_final, h_s, p_s, *, tm):
    m, d = h.shape
    pd = p.shape[1]
    nb = h_s.shape[0]
    assert nb <= tm
    once = pl.Buffered(1)
    const = lambda i: (0, 0)
    return pl.pallas_call(
        functools.partial(_ple_final_kernel, row_block=256, row_chunk=32),
        grid=(m // tm,),
        in_specs=[
            pl.BlockSpec((tm, d), lambda i: (i, 0)),
            pl.BlockSpec((tm, pd), lambda i: (i, 0)),
            pl.BlockSpec((1, d), const, pipeline_mode=once),
            pl.BlockSpec((d, d), const, pipeline_mode=once),
            pl.BlockSpec((pd, d), const, pipeline_mode=once),
            pl.BlockSpec((1, d), const, pipeline_mode=once),
            pl.BlockSpec((nb, d), const, pipeline_mode=once),
            pl.BlockSpec((nb, pd), const, pipeline_mode=once),
        ],
        out_specs=[pl.BlockSpec((tm, d), lambda i: (i, 0)), pl.BlockSpec((nb, d), const)],
        out_shape=[jax.ShapeDtypeStruct((m, d), F32), jax.ShapeDtypeStruct((nb, d), F32)],
        scratch_shapes=[pltpu.VMEM((tm, d), BF16), pltpu.VMEM((d, d), BF16),
                        pltpu.VMEM((pd, d), BF16)],
        compiler_params=_params(("arbitrary",)),
        name="ple_final",
    )(h, p, g_ple, w_gate, w_proj, g_final, h_s, p_s)


def _rope_tables(pos):
    inv = ROPE_THETA ** (-np.arange(0, HEAD_DIM, 2, dtype=np.float64) / HEAD_DIM)
    ang = np.asarray(pos, np.float64)[:, None] * inv[None, :]
    cos, sin = np.cos(ang), np.sin(ang)
    return (jnp.asarray(np.concatenate([cos, cos], axis=-1), F32),
            jnp.asarray(np.concatenate([-sin, sin], axis=-1), F32))


def _decay_tables(c):
    log_g = np.log1p(-np.exp2(-5.0 - np.arange(RET_HEADS, dtype=np.float64)))
    idx = np.arange(c, dtype=np.float64)
    diff = idx[:, None] - idx[None, :]
    dmask = np.where(diff >= 0, np.exp(log_g[:, None, None] * np.maximum(diff, 0.0)), 0.0)
    q_dec = np.exp(log_g[:, None] * (idx + 1.0))[..., None]
    k_dec = np.exp(log_g[:, None] * (c - 1.0 - idx))[..., None]
    c_dec = np.exp(log_g * c)[:, None, None]
    return tuple(jnp.asarray(t, F32) for t in (dmask, q_dec, k_dec, c_dec))


def kernel(x_prompt, x_sample, p_prompt, p_sample, state_ret, state_conv, g_attn, w_in, gm_ln_g,
           gm_ln_b, gm_ws, gm_bs, w_o, g_ffn, w_up, conv_w, conv_b, w_down, g_ple, w_ple_gate,
           w_ple_proj, g_final):
    batch, seq, d = x_prompt.shape
    nb = x_sample.shape[0]
    assert x_sample.shape[1] == 1 and g_attn.shape[0] == 1

    g_attn2, g_ffn2, g_ple2 = g_attn[0][None], g_ffn[0][None], g_ple[0][None]
    g_fin2 = g_final[None]
    lng, lnb = gm_ln_g[0][None], gm_ln_b[0][None]
    ws, bs = gm_ws[0], gm_bs[0]
    cw, cb = conv_w[0], conv_b[0][None]
    n_groups = GM_WIDTH // HEAD_DIM

    cos_p, sin_p = _rope_tables(np.arange(seq))
    dmask, q_dec, k_dec, c_dec = _decay_tables(CHUNK)
    qdec_p = jnp.broadcast_to(q_dec, (RET_HEADS, CHUNK, HEAD_DIM))
    kdec_p = jnp.broadcast_to(k_dec, (RET_HEADS, CHUNK, HEAD_DIM))
    cdec_p = jnp.broadcast_to(c_dec, (RET_HEADS, 1, HEAD_DIM))
    bsf = jnp.broadcast_to(bs.T[:, :, None], (CHUNK, n_groups, HEAD_DIM)).reshape(CHUNK, GM_WIDTH)
    cos_s, sin_s = _rope_tables(PAST_LEN + np.arange(1))
    dmask1, q_dec1, k_dec1, c_dec1 = _decay_tables(1)
    qdec_s = jnp.broadcast_to(q_dec1, (RET_HEADS, 1, HEAD_DIM))
    kdec_s = jnp.broadcast_to(k_dec1, (RET_HEADS, 1, HEAD_DIM))
    cdec_s = jnp.broadcast_to(c_dec1, (RET_HEADS, 1, HEAD_DIM))
    w00 = jnp.broadcast_to(ws[:, 0, 0][:, None], (n_groups, HEAD_DIM)).reshape(1, GM_WIDTH)
    bs0 = jnp.broadcast_to(bs[:, 0][:, None], (n_groups, HEAD_DIM)).reshape(1, GM_WIDTH)
    conv_taps = jnp.broadcast_to(jnp.concatenate([cw, cb], axis=0)[:, None, :], (4, 8, 2 * D_FF))

    xp = x_prompt.reshape(batch * seq, d)
    xs = x_sample.reshape(nb, d)

    z_s, w_in_b = _norm_matmul(xs, g_attn2, w_in[0], tn=1024, name="sample_in_proj")
    mix_p, ret_p = _in_mixer(xp, g_attn2, w_in_b, batch, seq, cos_p, sin_p, dmask, qdec_p, kdec_p,
                             cdec_p, lng, lnb, ws, bsf, tm=512)
    mix_s, ret_s, vn_s = _sample_mixer(z_s, state_ret[0], cos_s, sin_s, dmask1, qdec_s, kdec_s,
                                       cdec_s, lng, lnb, w00, bs0, tb=8)
    h1_p, xn2_p, h1_s, xn2_s = _attn_out(mix_p, w_o[0], xp, g_ffn2, mix_s, xs, tm=512)

    act_p, csg_p, csu_p, asg_s, asu_s, act_s = _ffn_up(
        xn2_p, w_up[0], conv_taps, xn2_s, state_conv[0], batch=batch, seq=seq, tm=1024, tn=512)
    h2_p, h2_s = _ffn_down(act_p, w_down[0], h1_p, act_s, h1_s, tm=512, tn=1024)

    y_p, y_s = _ple_final(h2_p, p_prompt[0].reshape(batch * seq, PLE_DIM), g_ple2, w_ple_gate[0],
                          w_ple_proj[0], g_fin2, h2_s, p_sample[0].reshape(nb, PLE_DIM), tm=512)

    conv_p = jnp.concatenate([csg_p, csu_p], axis=-1)[None]
    a_s = jnp.concatenate([asg_s, asu_s], axis=-1)
    conv_s = jnp.stack([state_conv[0][:, 1, :], a_s], axis=1)[None]
    return (y_p.reshape(batch, seq, d), y_s.reshape(nb, 1, d), ret_p[None], conv_p,
            ret_s[None], conv_s, vn_s.reshape(1, nb, 1, GM_WIDTH))
```
